```python
import jax, jax.numpy as jnp
from jax import lax
import numpy as np

D_MODEL = 1024
BATCH = 2
SEQ = 8192
DEPTH = 1
DEC_BATCH = 128
DEC_SEQ = 8
PAST_LEN = 8192
PAGE_SIZE = 128

ATT_HEAD_DIM = 64
ATT_GROUPS = ((128, 1), (512, 4), (2048, 16))
N_ATT_GROUPS = 3
ATT_HEADS_PER_GROUP = 4
ATT_HEADS = N_ATT_GROUPS * ATT_HEADS_PER_GROUP
ATT_WIDTH = ATT_HEADS * ATT_HEAD_DIM
ATT_OUT_WIDTH = ATT_HEADS_PER_GROUP * ATT_HEAD_DIM
ATT_BLOCK = 128
ALIBI_MAX_EXP = 8.0

HG_HEADS = 4
HG_DK = 128
HG_DV = 128
HG_KW = HG_HEADS * HG_DK
HG_VW = HG_HEADS * HG_DV
HG_CHUNK = 64

PEER_HEADS = 8
PEER_NKEYS = 128
PEER_EXPERTS = PEER_NKEYS * PEER_NKEYS
PEER_DHALF = 128
PEER_DQ = 2 * PEER_DHALF
PEER_TOPK = 16
PEER_BLOCK = 256

N_MOD = 6
NORM_EPS = 1e-6

IN_SPLITS = (ATT_WIDTH, ATT_WIDTH, ATT_WIDTH, HG_KW, HG_VW, HG_KW, HG_VW, D_MODEL, D_MODEL)
IN_COLS = sum(IN_SPLITS)

kernel_name = "hybrid_dilated_attn_hgrn2_peer_adaln_step"

F32 = jnp.float32


def rms_norm(x, g):
    xf = x.astype(F32)
    y = xf * lax.rsqrt(jnp.mean(xf * xf, axis=-1, keepdims=True) + NORM_EPS)
    return y.astype(x.dtype) * g


def alibi_slopes():
    return jnp.exp2(-ALIBI_MAX_EXP * jnp.arange(1, ATT_HEADS + 1, dtype=F32) / ATT_HEADS)


def dilated_attn_prompt(q, k, v, window, dil, slopes):
    B, S, H, Dh = q.shape
    n_taps = window // dil
    span = dil * ATT_BLOCK
    Sp = -(-S // span) * span
    L = Sp // dil
    nb = L // ATT_BLOCK

    def to_sub(a):
        a = jnp.pad(a, ((0, 0), (0, Sp - S), (0, 0), (0, 0)))
        a = a.reshape(B, L, dil, H, Dh).transpose(0, 2, 1, 3, 4)
        return a.reshape(B, dil, nb, ATT_BLOCK, H, Dh)

    def with_prev(a):
        prev = jnp.concatenate([jnp.zeros_like(a[:, :, :1]), a[:, :, :-1]], axis=2)
        return jnp.concatenate([prev, a], axis=3)

    def from_sub(a):
        rest = a.shape[5:]
        a = a.reshape((B, dil, L, H) + rest)
        a = jnp.moveaxis(a, 1, 2)
        return a.reshape((B, Sp, H) + rest)[:, :S]

    qb = to_sub(q)
    kk = with_prev(to_sub(k))
    vv = with_prev(to_sub(v))
    i = jnp.arange(ATT_BLOCK)[:, None]
    j = jnp.arange(2 * ATT_BLOCK)[None, :]
    delta = i + ATT_BLOCK - j
    key_sub = jnp.arange(nb)[:, None, None] * ATT_BLOCK + j[None] - ATT_BLOCK
    mask = (delta >= 0) & (delta <= n_taps) & (key_sub >= 0)
    bias = -slopes[:, None, None] * (delta * dil).astype(F32)[None]
    s = jnp.einsum('bgnqhd,bgnkhd->bgnhqk', qb, kk).astype(F32) * (Dh ** -0.5) + bias
    s = jnp.where(mask[:, None], s, -jnp.inf)
    lse = jax.nn.logsumexp(s, axis=-1)
    p = jnp.exp(s - lse[..., None]).astype(v.dtype)
    o = jnp.einsum('bgnhqk,bgnkhd->bgnqhd', p, vv)
    return from_sub(o), from_sub(jnp.swapaxes(lse, -1, -2))


def dilated_attn_sample(q, k, v, kbuf, vbuf, window, dil, slopes):
    Bd, T, H, Dh = q.shape
    W = kbuf.shape[1]
    n_taps = window // dil
    kall = jnp.concatenate([kbuf, k], axis=1)
    vall = jnp.concatenate([vbuf, v], axis=1)
    taps = jnp.arange(n_taps + 1)
    idx = W + jnp.arange(T)[:, None] - taps[None, :] * dil
    valid = idx >= 0
    idxc = jnp.clip(idx, 0, None)
    kg = jnp.take(kall, idxc, axis=1)
    vg = jnp.take(vall, idxc, axis=1)
    bias = -slopes[:, None] * (taps * dil).astype(F32)[None, :]
    s = jnp.einsum('bthd,btjhd->bthj', q, kg).astype(F32) * (Dh ** -0.5) + bias
    s = jnp.where(valid[:, None, :], s, -jnp.inf)
    lse = jax.nn.logsumexp(s, axis=-1)
    p = jnp.exp(s - lse[..., None]).astype(v.dtype)
    o = jnp.einsum('bthj,btjhd->bthd', p, vg)
    return o, lse, kall[:, T:], vall[:, T:]


def hgrn2_scan(q, k, v, log_f, s0):
    B, L, H, DK = q.shape
    DV = v.shape[-1]
    C = min(HG_CHUNK, L)
    Lp = -(-L // C) * C
    NC = Lp // C

    def chunks(a):
        a = jnp.pad(a.astype(F32), ((0, 0), (0, Lp - L), (0, 0), (0, 0)))
        return jnp.moveaxis(a.reshape(B, NC, C, H, a.shape[-1]), 1, 0)

    causal = jnp.tril(jnp.ones((C, C), dtype=bool))[None, :, :, None, None]

    def step(S, inp):
        qc, kc, vc, lfc = inp
        b = jnp.cumsum(lfc, axis=1)
        o_inter = jnp.einsum('bthk,bhkv->bthv', qc * jnp.exp(b), S)
        diff = b[:, :, None] - b[:, None, :]
        dec = jnp.exp(jnp.where(causal, diff, -jnp.inf))
        A = jnp.einsum('bthk,bshk,btshk->bhts', qc, kc, dec)
        o_intra = jnp.einsum('bhts,bshv->bthv', A, vc)
        bC = b[:, -1]
        kd = kc * jnp.exp(bC[:, None] - b)
        S_new = jnp.exp(bC)[..., None] * S + jnp.einsum('bshk,bshv->bhkv', kd, vc)
        return S_new, o_inter + o_intra

    S_fin, o = lax.scan(step, s0.astype(F32), (chunks(q), chunks(k), chunks(v), chunks(log_f)))
    o = jnp.moveaxis(o, 0, 1).reshape(B, Lp, H, DV)[:, :L]
    return o, S_fin.astype(s0.dtype)


def token_mixer(h, w_in, w_att_branch, w_hg_branch, w_out, lb, hg_norm_g, slopes, att_caches, s0):
    N, L, _ = h.shape
    proj = h @ w_in
    cuts = [int(c) for c in np.cumsum(IN_SPLITS)[:-1]]
    q, k, v, zf, zi, zq, zg, za, zh = jnp.split(proj, cuts, axis=-1)
    q = q.reshape(N, L, ATT_HEADS, ATT_HEAD_DIM)
    k = k.reshape(N, L, ATT_HEADS, ATT_HEAD_DIM)
    v = v.reshape(N, L, ATT_HEADS, ATT_HEAD_DIM)

    outs, lses, bufs = [], [], []
    for gi, (window, dil) in enumerate(ATT_GROUPS):
        hs = slice(gi * ATT_HEADS_PER_GROUP, (gi + 1) * ATT_HEADS_PER_GROUP)
        if att_caches is None:
            o, lse = dilated_attn_prompt(q[:, :, hs], k[:, :, hs], v[:, :, hs], window, dil, slopes[hs])
            keep = min(window, L)
            kb, vb = k[:, L - keep:, hs], v[:, L - keep:, hs]
        else:
            o, lse, kb, vb = dilated_attn_sample(q[:, :, hs], k[:, :, hs], v[:, :, hs],
                                                 att_caches[2 * gi], att_caches[2 * gi + 1],
                                                 window, dil, slopes[hs])
        outs.append(o)
        lses.append(lse)
        bufs.append(kb)
        bufs.append(vb)
    wg = jax.nn.softmax(jnp.stack(lses, axis=0), axis=0)
    att = jnp.einsum('gnlh,gnlhd->nlhd', wg.astype(h.dtype), jnp.stack(outs, axis=0))
    att = att.reshape(N, L, ATT_OUT_WIDTH)

    f = lb + (1.0 - lb) * jax.nn.sigmoid(zf.astype(F32))
    log_f = jnp.log(f).reshape(N, L, HG_HEADS, HG_DK)
    kh = (1.0 - f).reshape(N, L, HG_HEADS, HG_DK)
    o_h, s_fin = hgrn2_scan(zq.reshape(N, L, HG_HEADS, HG_DK), kh,
                            zi.reshape(N, L, HG_HEADS, HG_DV), log_f, s0)
    o_h = rms_norm(o_h.astype(h.dtype), hg_norm_g) * jax.nn.silu(zg.reshape(N, L, HG_HEADS, HG_DV))
    o_h = o_h.reshape(N, L, HG_VW)

    merged = jax.nn.sigmoid(za) * (att @ w_att_branch) + jax.nn.sigmoid(zh) * (o_h @ w_hg_branch)
    return merged @ w_out, bufs, s_fin


def peer(h, wq, subkeys, u_tab, v_tab):
    N, D = h.shape
    blk = min(PEER_BLOCK, N)
    Np = -(-N // blk) * blk
    hb = jnp.pad(h, ((0, Np - N), (0, 0))).reshape(Np // blk, blk, D)

    def one(xb):
        qry = (xb @ wq).reshape(blk, PEER_HEADS, 2, PEER_DHALF)
        sc = jnp.einsum('thpd,hpkd->thpk', qry, subkeys).astype(F32)
        s1, i1 = lax.top_k(sc[:, :, 0], PEER_TOPK)
        s2, i2 = lax.top_k(sc[:, :, 1], PEER_TOPK)
        cand = (s1[..., :, None] + s2[..., None, :]).reshape(blk, PEER_HEADS, PEER_TOPK * PEER_TOPK)
        sf, ci = lax.top_k(cand, PEER_TOPK)
        e1 = jnp.take_along_axis(i1, ci // PEER_TOPK, axis=-1)
        e2 = jnp.take_along_axis(i2, ci % PEER_TOPK, axis=-1)
        eid = e1 * PEER_NKEYS + e2
        g = jax.nn.softmax(sf, axis=-1)
        ug = jnp.take(u_tab, eid, axis=0)
        vg = jnp.take(v_tab, eid, axis=0)
        act = jax.nn.gelu(jnp.einsum('td,thkd->thk', xb, ug).astype(F32))
        return jnp.einsum('thk,thkd->td', (g * act).astype(xb.dtype), vg)

    return lax.map(one, hb).reshape(Np, D)[:N]


def layer(x, c, ada_w, ada_b, norm1_g, w_in, w_att_branch, w_hg_branch, w_out, lb, hg_norm_g,
          norm2_g, peer_wq, peer_subkeys, peer_u, peer_v, slopes, att_caches, s0):
    mod = jax.nn.silu(c) @ ada_w + ada_b
    sh1, sc1, g1, sh2, sc2, g2 = jnp.split(mod, N_MOD, axis=-1)
    h = rms_norm(x, norm1_g) * (1.0 + sc1[:, None]) + sh1[:, None]
    mix, bufs, s_fin = token_mixer(h, w_in, w_att_branch, w_hg_branch, w_out, lb, hg_norm_g,
                                   slopes, att_caches, s0)
    x = x + g1[:, None] * mix
    h2 = rms_norm(x, norm2_g) * (1.0 + sc2[:, None]) + sh2[:, None]
    N, L, D = x.shape
    ff = peer(h2.reshape(N * L, D), peer_wq, peer_subkeys, peer_u, peer_v).reshape(N, L, D)
    x = x + g2[:, None] * ff
    return x, bufs, s_fin


def setup_inputs(seed: int = 0) -> dict:
    key = jax.random.key(seed)
    ks = jax.random.split(key, 32)

    def nrm(k, shape, scale):
        return jax.random.normal(k, shape, F32) * scale

    def buf_shape(window):
        return (DEPTH, DEC_BATCH, min(window, PAST_LEN), ATT_HEADS_PER_GROUP, ATT_HEAD_DIM)

    D = D_MODEL
    return {
        "x_prompt": nrm(ks[0], (BATCH, SEQ, D), 1.0),
        "x_sample": nrm(ks[1], (DEC_BATCH, DEC_SEQ, D), 1.0),
        "cache_k0": nrm(ks[2], buf_shape(ATT_GROUPS[0][0]), 1.0),
        "cache_v0": nrm(ks[3], buf_shape(ATT_GROUPS[0][0]), 1.0),
        "cache_k1": nrm(ks[4], buf_shape(ATT_GROUPS[1][0]), 1.0),
        "cache_v1": nrm(ks[5], buf_shape(ATT_GROUPS[1][0]), 1.0),
        "cache_k2": nrm(ks[6], buf_shape(ATT_GROUPS[2][0]), 1.0),
        "cache_v2": nrm(ks[7], buf_shape(ATT_GROUPS[2][0]), 1.0),
        "state_hgrn": nrm(ks[8], (DEPTH, DEC_BATCH, HG_HEADS, HG_DK, HG_DV), 0.3),
        "c_prompt": nrm(ks[9], (BATCH, D), 1.0),
        "c_sample": nrm(ks[10], (DEC_BATCH, D), 1.0),
        "ada_w": nrm(ks[11], (DEPTH, D, N_MOD * D), 0.5 * D ** -0.5),
        "ada_b": nrm(ks[12], (DEPTH, N_MOD * D), 0.02),
        "norm1_g": 1.0 + nrm(ks[13], (DEPTH, D), 0.02),
        "w_in": nrm(ks[14], (DEPTH, D, IN_COLS), D ** -0.5),
        "w_att_branch": nrm(ks[15], (DEPTH, ATT_OUT_WIDTH, D), ATT_OUT_WIDTH ** -0.5),
        "w_hg_branch": nrm(ks[16], (DEPTH, HG_VW, D), HG_VW ** -0.5),
        "w_out": nrm(ks[17], (DEPTH, D, D), D ** -0.5),
        "hg_lower_bound": nrm(ks[18], (DEPTH + 1, HG_KW), 0.1),
        "hg_norm_g": 1.0 + nrm(ks[19], (DEPTH, HG_DV), 0.02),
        "norm2_g": 1.0 + nrm(ks[20], (DEPTH, D), 0.02),
        "peer_wq": nrm(ks[21], (DEPTH, D, PEER_HEADS * PEER_DQ), D ** -0.5),
        "peer_subkeys": nrm(ks[22], (DEPTH, PEER_HEADS, 2, PEER_NKEYS, PEER_DHALF), PEER_DHALF ** -0.5),
        "peer_u": nrm(ks[23], (DEPTH, PEER_EXPERTS, D), D ** -0.5),
        "peer_v": nrm(ks[24], (DEPTH, PEER_EXPERTS, D), 0.5),
        "final_norm_g": 1.0 + nrm(ks[25], (D,), 0.02),
    }


def reference(x_prompt, x_sample, cache_k0, cache_v0, cache_k1, cache_v1, cache_k2, cache_v2,
              state_hgrn, c_prompt, c_sample, ada_w, ada_b, norm1_g, w_in, w_att_branch,
              w_hg_branch, w_out, hg_lower_bound, hg_norm_g, norm2_g, peer_wq, peer_subkeys,
              peer_u, peer_v, final_norm_g):
    slopes = alibi_slopes()
    lbs = jnp.cumsum(jax.nn.softmax(hg_lower_bound.astype(F32), axis=0), axis=0)
    yp, ys = x_prompt, x_sample
    new_p = [[] for _ in range(7)]
    new_s = [[] for _ in range(7)]
    for l in range(DEPTH):
        wl = (ada_w[l], ada_b[l], norm1_g[l], w_in[l], w_att_branch[l], w_hg_branch[l], w_out[l],
              lbs[l], hg_norm_g[l], norm2_g[l], peer_wq[l], peer_subkeys[l], peer_u[l], peer_v[l], slopes)
        s0_p = jnp.zeros((yp.shape[0], HG_HEADS, HG_DK, HG_DV), F32)
        yp, bufs_p, st_p = layer(yp, c_prompt, *wl, None, s0_p)
        caches_l = (cache_k0[l], cache_v0[l], cache_k1[l], cache_v1[l], cache_k2[l], cache_v2[l])
        ys, bufs_s, st_s = layer(ys, c_sample, *wl, caches_l, state_hgrn[l])
        for i in range(6):
            new_p[i].append(bufs_p[i])
            new_s[i].append(bufs_s[i])
        new_p[6].append(st_p)
        new_s[6].append(st_s)
    y_prompt = rms_norm(yp, final_norm_g)
    y_sample = rms_norm(ys, final_norm_g)
    k0_p = jnp.stack(new_p[0]); v0_p = jnp.stack(new_p[1])
    k1_p = jnp.stack(new_p[2]); v1_p = jnp.stack(new_p[3])
    k2_p = jnp.stack(new_p[4]); v2_p = jnp.stack(new_p[5])
    hgrn_p = jnp.stack(new_p[6])
    k0_s = jnp.stack(new_s[0]); v0_s = jnp.stack(new_s[1])
    k1_s = jnp.stack(new_s[2]); v1_s = jnp.stack(new_s[3])
    k2_s = jnp.stack(new_s[4]); v2_s = jnp.stack(new_s[5])
    hgrn_s = jnp.stack(new_s[6])
    return (y_prompt, y_sample, k0_p, v0_p, k1_p, v1_p, k2_p, v2_p, hgrn_p,
            k0_s, v0_s, k1_s, v1_s, k2_s, v2_s, hgrn_s)
```

```python
import functools

import jax
import jax.numpy as jnp
from jax import lax
from jax.experimental import pallas as pl
from jax.experimental.pallas import tpu as pltpu

F32 = jnp.float32
BF16 = jnp.bfloat16
NEG_INF = float("-inf")

NORM_EPS = 1e-6
N_MOD = 6

ATT_HEAD_DIM = 64
ATT_GROUPS = ((128, 1), (512, 4), (2048, 16))
ATT_HEADS_PER_GROUP = 4
ATT_HEADS = len(ATT_GROUPS) * ATT_HEADS_PER_GROUP
ATT_GROUP_WIDTH = ATT_HEADS_PER_GROUP * ATT_HEAD_DIM
ATT_BLOCK = 128
ALIBI_MAX_EXP = 8.0

HG_HEADS = 4
HG_DK = 128
HG_DV = 128
HG_CHUNK = 64
HG_SUB = 8
HG_HEADS_PER_STEP = 2

PEER_HEADS = 8
PEER_NKEYS = 128
PEER_DHALF = 128
PEER_TOPK = 16

COL_ZA, COL_ZH = 0, 1
COL_Q, COL_K, COL_V = 8, 11, 14
COL_ZF, COL_ZI, COL_ZQ, COL_ZG = 17, 19, 21, 23
ORIG_GATE_START = 4352

VMEM_LIMIT = 48 * 1024 * 1024


def _bdot(a, b):
    return jnp.dot(a.astype(BF16), b.astype(BF16), preferred_element_type=F32)


def _bdot_nt(a, b):
    return lax.dot_general(a.astype(BF16), b.astype(BF16), (((1,), (1,)), ((), ())),
                           preferred_element_type=F32)


def _log2(k):
    assert k > 0 and k & (k - 1) == 0, k
    return k.bit_length() - 1


def _div(x, k):
    return x >> _log2(k)


def _mod(x, k):
    return x & ((1 << _log2(k)) - 1)


def _rms(x):
    return x * lax.rsqrt(jnp.mean(x * x, axis=-1, keepdims=True) + NORM_EPS)


def _params(*sem):
    return pltpu.CompilerParams(dimension_semantics=sem, vmem_limit_bytes=VMEM_LIMIT)


def _adaln_kernel(c_ref, w_ref, b_ref, o_ref):
    c = c_ref[...]
    o_ref[...] = _bdot(c * jax.nn.sigmoid(c), w_ref[...]) + b_ref[...]


def _adaln(c, ada_w, ada_b):
    n, d = c.shape
    cols = ada_w.shape[1]
    tn = 1536
    return pl.pallas_call(
        _adaln_kernel,
        grid=(cols // tn,),
        in_specs=[pl.BlockSpec((n, d), lambda j: (0, 0)),
                  pl.BlockSpec((d, tn), lambda j: (0, j)),
                  pl.BlockSpec((1, tn), lambda j: (0, j))],
        out_specs=pl.BlockSpec((n, tn), lambda j: (0, j)),
        out_shape=jax.ShapeDtypeStruct((n, cols), F32),
        compiler_params=_params("arbitrary"),
    )(c, ada_w, ada_b.reshape(1, cols))


def _inproj_kernel(x_ref, sh_ref, sc_ref, g_ref, w_ref, o_ref, h_ref):
    nb, lb, d = x_ref.shape

    @pl.when(pl.program_id(2) == 0)
    def _():
        h = _rms(x_ref[...]) * g_ref[...] * (1.0 + sc_ref[...]) + sh_ref[...]
        h_ref[...] = h.reshape(nb * lb, d).astype(BF16)

    o = jnp.dot(h_ref[...], w_ref[...], preferred_element_type=F32)
    o_ref[...] = o.reshape(nb, lb, o.shape[-1])


def _token_blocks(n, l):
    if l >= 512:
        return 1, 512
    return min(n, 512 // l), l


def _inproj(x, mod3, norm_g, w_bf16):
    n, l, d = x.shape
    cols = w_bf16.shape[1]
    nb, lb = _token_blocks(n, l)
    tn = 1280
    return pl.pallas_call(
        _inproj_kernel,
        grid=(n // nb, l // lb, cols // tn),
        in_specs=[pl.BlockSpec((nb, lb, d), lambda i, t, j: (i, t, 0)),
                  pl.BlockSpec((nb, 1, d), lambda i, t, j: (i, 0, 0)),
                  pl.BlockSpec((nb, 1, d), lambda i, t, j: (i, 0, 1)),
                  pl.BlockSpec((1, d), lambda i, t, j: (0, 0)),
                  pl.BlockSpec((d, tn), lambda i, t, j: (0, j))],
        out_specs=pl.BlockSpec((nb, lb, tn), lambda i, t, j: (i, t, j)),
        out_shape=jax.ShapeDtypeStruct((n, l, cols), F32),
        scratch_shapes=[pltpu.VMEM((nb * lb, d), BF16)],
        compiler_params=_params("arbitrary", "arbitrary", "arbitrary"),
    )(x, mod3, mod3, norm_g.reshape(1, d), w_bf16)


def _alibi_slope(head):
    return 2.0 ** (-ALIBI_MAX_EXP * (head + 1) / ATT_HEADS)


def _attn_prompt_kernel(q_ref, kp_ref, kc_ref, vp_ref, vc_ref, o_ref, lse_ref, *, dil, n_taps, group):
    blk = ATT_BLOCK
    n = pl.program_id(2)
    q = q_ref[0]
    kk = jnp.concatenate([kp_ref[0], kc_ref[0]], axis=0)
    vv = jnp.concatenate([vp_ref[0], vc_ref[0]], axis=0)
    i = lax.broadcasted_iota(jnp.int32, (blk, 2 * blk), 0)
    j = lax.broadcasted_iota(jnp.int32, (blk, 2 * blk), 1)
    delta = i + blk - j
    valid = (delta >= 0) & (delta <= n_taps) & ((j >= blk) | (n > 0))
    dist = (delta * dil).astype(F32)
    outs, lses = [], []
    for h in range(ATT_HEADS_PER_GROUP):
        cs = slice(h * ATT_HEAD_DIM, (h + 1) * ATT_HEAD_DIM)
        slope = _alibi_slope(group * ATT_HEADS_PER_GROUP + h)
        s = _bdot_nt(q[:, cs], kk[:, cs]) * (ATT_HEAD_DIM ** -0.5) - slope * dist
        s = jnp.where(valid, s, NEG_INF)
        m = jnp.max(s, axis=-1, keepdims=True)
        p = jnp.exp(s - m)
        l = jnp.sum(p, axis=-1, keepdims=True)
        outs.append(_bdot(p, vv[:, cs]) / l)
        lses.append(jnp.broadcast_to(m + jnp.log(l), (blk, ATT_HEAD_DIM)))
    o_ref[0] = jnp.concatenate(outs, axis=1)
    lse_ref[0] = jnp.concatenate(lses, axis=1)


def _attn_prompt(proj, group):
    window, dil = ATT_GROUPS[group]
    b, s, cols = proj.shape
    gw = ATT_GROUP_WIDTH
    assert s % (dil * ATT_BLOCK) == 0 and cols % gw == 0
    sub_len = s // dil
    nblk = sub_len // ATT_BLOCK
    cpr = cols // gw
    view = proj.reshape(b, sub_len, dil * cols)

    def spec(col, prev):
        if prev:
            return pl.BlockSpec((1, ATT_BLOCK, gw), lambda bb, r, n: (bb, jnp.maximum(n - 1, 0), r * cpr + col + group))
        return pl.BlockSpec((1, ATT_BLOCK, gw), lambda bb, r, n: (bb, n, r * cpr + col + group))

    out_spec = pl.BlockSpec((1, ATT_BLOCK, gw), lambda bb, r, n: (bb, n, r))
    o, lse = pl.pallas_call(
        functools.partial(_attn_prompt_kernel, dil=dil, n_taps=window // dil, group=group),
        grid=(b, dil, nblk),
        in_specs=[spec(COL_Q, False), spec(COL_K, True), spec(COL_K, False), spec(COL_V, True), spec(COL_V, False)],
        out_specs=[out_spec, out_spec],
        out_shape=[jax.ShapeDtypeStruct((b, sub_len, dil * gw), F32)] * 2,
        compiler_params=_params("arbitrary", "arbitrary", "arbitrary"),
    )(view, view, view, view, view)
    return o.reshape(b, s, gw), lse.reshape(b, s, gw)


def _attn_sample_kernel(q_ref, kn_ref, vn_ref, ck_ref, cv_ref, o_ref, lse_ref, nk_ref, nv_ref, *, window, dil, group):
    t_new = q_ref.shape[1]
    w = ck_ref.shape[1]
    gw = ATT_GROUP_WIDTH
    nh = ATT_HEADS_PER_GROUP
    rows = nh * t_new
    q, kn, vn = q_ref[0], kn_ref[0], vn_ref[0]
    ck, cv = ck_ref[0], cv_ref[0]

    r = lax.broadcasted_iota(jnp.int32, (rows, gw), 0)
    c = lax.broadcasted_iota(jnp.int32, (rows, gw), 1)
    head_mask = _div(r, t_new) == _div(c, ATT_HEAD_DIM)
    qbd = jnp.where(head_mask, jnp.concatenate([q] * nh, axis=0), 0.0)
    s_c = _bdot_nt(qbd, ck) * (ATT_HEAD_DIM ** -0.5)
    s_n = _bdot_nt(qbd, kn) * (ATT_HEAD_DIM ** -0.5)

    hrow = _div(lax.broadcasted_iota(jnp.int32, (rows, 1), 0), t_new)
    slope = jnp.zeros((rows, 1), F32)
    for h in range(nh):
        slope = jnp.where(hrow == h, _alibi_slope(group * nh + h), slope)

    def masked(s, first_pos):
        tq = _mod(lax.broadcasted_iota(jnp.int32, s.shape, 0), t_new)
        dist = w + tq - (first_pos + lax.broadcasted_iota(jnp.int32, s.shape, 1))
        valid = (dist >= 0) & (_mod(dist, dil) == 0) & (dist <= window)
        return jnp.where(valid, s - slope * dist.astype(F32), NEG_INF)

    s_c = masked(s_c, 0)
    s_n = masked(s_n, w)
    m = jnp.maximum(jnp.max(s_c, axis=-1, keepdims=True), jnp.max(s_n, axis=-1, keepdims=True))
    p_c = jnp.exp(s_c - m)
    p_n = jnp.exp(s_n - m)
    l = jnp.sum(p_c, axis=-1, keepdims=True) + jnp.sum(p_n, axis=-1, keepdims=True)
    o_full = jnp.where(head_mask, (_bdot(p_c, cv) + _bdot(p_n, vn)) / l, 0.0)
    lse_full = jnp.where(head_mask, m + jnp.log(l), 0.0)
    o, lse = o_full[0:t_new], lse_full[0:t_new]
    for h in range(1, nh):
        o = o + o_full[h * t_new:(h + 1) * t_new]
        lse = lse + lse_full[h * t_new:(h + 1) * t_new]
    o_ref[0] = o
    lse_ref[0] = lse

    nk_ref[0, 0:w - t_new, :] = ck[t_new:w]
    nk_ref[0, w - t_new:w, :] = kn
    nv_ref[0, 0:w - t_new, :] = cv[t_new:w]
    nv_ref[0, w - t_new:w, :] = vn


def _attn_sample(proj, cache_k, cache_v, group):
    window, dil = ATT_GROUPS[group]
    bd, t_new, _ = proj.shape
    w = cache_k.shape[1]
    gw = ATT_GROUP_WIDTH
    assert w >= window and w > t_new and t_new % 8 == 0

    def pspec(col):
        return pl.BlockSpec((1, t_new, gw), lambda bb: (bb, 0, col + group))

    cspec = pl.BlockSpec((1, w, gw), lambda bb: (bb, 0, 0))
    ospec = pl.BlockSpec((1, t_new, gw), lambda bb: (bb, 0, 0))
    return pl.pallas_call(
        functools.partial(_attn_sample_kernel, window=window, dil=dil, group=group),
        grid=(bd,),
        in_specs=[pspec(COL_Q), pspec(COL_K), pspec(COL_V), cspec, cspec],
        out_specs=[ospec, ospec, cspec, cspec],
        out_shape=[jax.ShapeDtypeStruct((bd, t_new, gw), F32)] * 2 + [jax.ShapeDtypeStruct((bd, w, gw), F32)] * 2,
        compiler_params=_params("arbitrary"),
    )(proj, proj, proj, cache_k, cache_v)


def _hgrn_levels(chunk):
    levels, m = [], HG_SUB
    while m < chunk:
        levels.append(m)
        m *= 2
    return tuple(levels)


def _pad_rows(a, rows):
    if a.shape[0] == rows:
        return a
    return jnp.concatenate([a, jnp.zeros((rows - a.shape[0], a.shape[1]), a.dtype)], axis=0)


def _hgrn_kernel(zf_ref, zi_ref, zq_ref, zg_ref, lbraw_ref, gn_ref, s0_ref, o_ref, sfin_ref, st_ref, *, layer):
    chunk = zf_ref.shape[1]
    levels = _hgrn_levels(chunk)
    cidx = pl.program_id(2)
    dk, dv = HG_DK, HG_DV

    @pl.when(cidx == 0)
    def _():
        for hh in range(HG_HEADS_PER_STEP):
            st_ref[hh] = s0_ref[0, hh].T

    raw = lbraw_ref[...]
    e = jnp.exp(raw - jnp.max(raw, axis=0, keepdims=True))
    lb_all = jnp.sum(e[0:layer + 1], axis=0, keepdims=True) / jnp.sum(e, axis=0, keepdims=True)

    rr = lax.broadcasted_iota(jnp.int32, (chunk, chunk), 0)
    cc = lax.broadcasted_iota(jnp.int32, (chunk, chunk), 1)
    sels = [cc <= rr] + [cc <= _div(rr, 2 * m) * (2 * m) + m - 1 for m in levels]
    sel = jnp.concatenate([jnp.where(sm, 1.0, 0.0) for sm in sels], axis=0).astype(BF16)

    trow = lax.broadcasted_iota(jnp.int32, (chunk, dk), 0)
    for hh in range(HG_HEADS_PER_STEP):
        cs = slice(hh * dk, (hh + 1) * dk)
        lb = lb_all[:, cs]
        f = lb + (1.0 - lb) * jax.nn.sigmoid(zf_ref[0][:, cs])
        lf = jnp.log(f)
        kh = 1.0 - f
        q = zq_ref[0][:, cs]
        v = zi_ref[0][:, cs]

        hi = lf.astype(BF16)
        r1 = lf - hi.astype(F32)
        mid = r1.astype(BF16)
        lo = (r1 - mid.astype(F32)).astype(BF16)
        bb = jnp.dot(sel, jnp.concatenate([hi, mid, lo], axis=1), preferred_element_type=F32)
        bb = bb[:, 0:dk] + bb[:, dk:2 * dk] + bb[:, 2 * dk:3 * dk]
        b = bb[0:chunk]

        st = st_ref[hh]
        o = _bdot_nt(q * jnp.exp(b), st)

        for off in range(HG_SUB):
            if off == 0:
                a = jnp.sum(q * kh, axis=-1, keepdims=True)
                o = o + a * v
            else:
                in_block = _mod(trow, HG_SUB) >= off
                dec = jnp.exp(jnp.where(in_block, b - pltpu.roll(b, off, 0), NEG_INF))
                a = jnp.sum(q * pltpu.roll(kh, off, 0) * dec, axis=-1, keepdims=True)
                o = o + a * pltpu.roll(v, off, 0)

        if levels:
            ti = lax.broadcasted_iota(jnp.int32, (chunk, chunk), 0)
            si = lax.broadcasted_iota(jnp.int32, (chunk, chunk), 1)
            amat = jnp.zeros((chunk, chunk), F32)
            for li, m in enumerate(levels):
                bm = bb[(li + 1) * chunk:(li + 2) * chunk]
                second = _mod(_div(trow, m), 2) == 1
                qs = q * jnp.exp(jnp.where(second, b - bm, NEG_INF))
                ks = kh * jnp.exp(jnp.where(second, NEG_INF, bm - b))
                amat = amat + jnp.where(_div(ti, 2 * m) == _div(si, 2 * m), _bdot_nt(qs, ks), 0.0)
            o = o + _bdot(amat, v)

        b_end = b[chunk - 1:chunk]
        kd = kh * jnp.exp(b_end - b)
        vt = _pad_rows(v, dv).T
        st_ref[hh] = st * jnp.exp(b_end) + _bdot(vt, _pad_rows(kd, dv))

        zg = zg_ref[0][:, cs]
        o_ref[0, :, cs] = _rms(o) * gn_ref[...] * (zg * jax.nn.sigmoid(zg))

    @pl.when(cidx == pl.num_programs(2) - 1)
    def _():
        for hh in range(HG_HEADS_PER_STEP):
            sfin_ref[0, hh] = st_ref[hh].T


def _hgrn(proj, hg_lower_bound, hg_norm_g, s0, layer):
    b, l, _ = proj.shape
    assert HG_DK == HG_DV and HG_DV == 128
    chunk = min(HG_CHUNK, l)
    assert l % chunk == 0 and chunk % HG_SUB == 0
    hps = HG_HEADS_PER_STEP
    wblk = hps * HG_DK

    def pspec(col):
        return pl.BlockSpec((1, chunk, wblk), lambda bb, hp, c: (bb, c, col + hp))

    sspec = pl.BlockSpec((1, hps, HG_DK, HG_DV), lambda bb, hp, c: (bb, hp, 0, 0))
    return pl.pallas_call(
        functools.partial(_hgrn_kernel, layer=layer),
        grid=(b, HG_HEADS // hps, l // chunk),
        in_specs=[pspec(COL_ZF), pspec(COL_ZI), pspec(COL_ZQ), pspec(COL_ZG),
                  pl.BlockSpec((hg_lower_bound.shape[0], wblk), lambda bb, hp, c: (0, hp)),
                  pl.BlockSpec((1, HG_DV), lambda bb, hp, c: (0, 0)),
                  sspec],
        out_specs=[pl.BlockSpec((1, chunk, wblk), lambda bb, hp, c: (bb, c, hp)), sspec],
        out_shape=[jax.ShapeDtypeStruct((b, l, HG_HEADS * HG_DV), F32),
                   jax.ShapeDtypeStruct(s0.shape, F32)],
        scratch_shapes=[pltpu.VMEM((hps, HG_DV, HG_DK), F32)],
        compiler_params=_params("arbitrary", "arbitrary", "arbitrary"),
    )(proj, proj, proj, proj, hg_lower_bound, hg_norm_g.reshape(1, HG_DV), s0)


def _merge_kernel(x_ref, o0_ref, l0_ref, o1_ref, l1_ref, o2_ref, l2_ref, oh_ref, za_ref, zh_ref,
                  g1_ref, sh2_ref, sc2_ref, n2_ref, wa_ref, wh_ref, wo_ref, x1_ref, h2_ref):
    nb, lb, d = x_ref.shape
    tm = nb * lb

    def flat(ref):
        return ref[...].reshape(tm, ref.shape[-1])

    l0, l1, l2 = flat(l0_ref), flat(l1_ref), flat(l2_ref)
    m = jnp.maximum(jnp.maximum(l0, l1), l2)
    e0, e1, e2 = jnp.exp(l0 - m), jnp.exp(l1 - m), jnp.exp(l2 - m)
    att = (e0 * flat(o0_ref) + e1 * flat(o1_ref) + e2 * flat(o2_ref)) / (e0 + e1 + e2)
    merged = (jax.nn.sigmoid(flat(za_ref)) * _bdot(att, wa_ref[...])
              + jax.nn.sigmoid(flat(zh_ref)) * _bdot(flat(oh_ref), wh_ref[...]))
    mix = _bdot(merged, wo_ref[...])
    x1 = x_ref[...] + g1_ref[...] * mix.reshape(nb, lb, d)
    x1_ref[...] = x1
    h2 = _rms(x1) * n2_ref[...] * (1.0 + sc2_ref[...]) + sh2_ref[...]
    h2_ref[...] = h2.reshape(tm, d).astype(BF16)


def _merge(x, att_o, att_lse, o_h, proj, mod3, norm2_g, wa, wh, wo):
    n, l, d = x.shape
    nb, lb = _token_blocks(n, l)
    gw = ATT_GROUP_WIDTH
    tpb = l // lb

    def tok(width, col=0):
        return pl.BlockSpec((nb, lb, width), lambda i, t: (i, t, col))

    def modspec(col):
        return pl.BlockSpec((nb, 1, d), lambda i, t: (i, 0, col))

    def full(a):
        return pl.BlockSpec(a.shape, lambda i, t: (0, 0))

    att_specs, att_args = [], []
    for o, lse in zip(att_o, att_lse):
        att_specs += [tok(gw), tok(gw)]
        att_args += [o, lse]
    return pl.pallas_call(
        _merge_kernel,
        grid=(n // nb, tpb),
        in_specs=[tok(d)] + att_specs + [tok(o_h.shape[-1]), tok(d, COL_ZA), tok(d, COL_ZH),
                                         modspec(2), modspec(3), modspec(4),
                                         pl.BlockSpec((1, d), lambda i, t: (0, 0)), full(wa), full(wh), full(wo)],
        out_specs=[tok(d), pl.BlockSpec((nb * lb, d), lambda i, t: (i * tpb + t, 0))],
        out_shape=[jax.ShapeDtypeStruct((n, l, d), F32), jax.ShapeDtypeStruct((n * l, d), BF16)],
        compiler_params=_params("arbitrary", "arbitrary"),
    )(x, *att_args, o_h, proj, proj, mod3, mod3, mod3, norm2_g.reshape(1, d), wa, wh, wo)


def _top16(s, vals):
    rows, cols = s.shape
    row = lax.broadcasted_iota(jnp.int32, (rows, cols), 0)
    krow = lax.broadcasted_iota(jnp.int32, (PEER_TOPK, cols), 0)

    def body(i, carry):
        s, rank, vals, idxs = carry
        m = jnp.max(s, axis=0, keepdims=True)
        idx = jnp.min(jnp.where(s == m, row, rows), axis=0, keepdims=True)
        hit = row == idx
        s = jnp.where(hit, NEG_INF, s)
        rank = jnp.where(hit, i.astype(F32), rank)
        vals = jnp.where(krow == i, m, vals)
        idxs = jnp.where(krow == i, idx, idxs)
        return s, rank, vals, idxs

    init = (s, jnp.full((rows, cols), float(PEER_TOPK), F32), vals, jnp.zeros((PEER_TOPK, cols), jnp.int32))
    _, rank, vals, idxs = lax.fori_loop(0, PEER_TOPK, body, init)
    return vals, rank, idxs


def _peer_select_kernel(h2_ref, wq_ref, sk_ref, r2_ref, t1_ref, a1_ref, a2_ref):
    tm = h2_ref.shape[0]
    kk = PEER_TOPK
    qry = jnp.dot(h2_ref[...], wq_ref[...], preferred_element_type=F32).astype(BF16)
    zeros = jnp.zeros((kk, tm), F32)
    krow = lax.broadcasted_iota(jnp.int32, (kk, tm), 0)
    for h in range(PEER_HEADS):
        c0 = h * 2 * PEER_DHALF
        s1 = _bdot_nt(sk_ref[h, 0], qry[:, c0:c0 + PEER_DHALF])
        s2 = _bdot_nt(sk_ref[h, 1], qry[:, c0 + PEER_DHALF:c0 + 2 * PEER_DHALF])
        v1, rank1, _ = _top16(s1, zeros)
        v2, rank2, _ = _top16(s2, zeros)
        cand = jnp.concatenate([v1[a:a + 1] + v2 for a in range(kk)], axis=0)
        sf, _, ci = _top16(cand, zeros)
        first = _div(ci, kk)
        z = jnp.sum(jnp.exp(sf - sf[0:1]), axis=0, keepdims=True)
        t1 = jnp.zeros_like(s1)
        for a in range(kk):
            count = jnp.sum(jnp.where(first == a, 1.0, 0.0), axis=0, keepdims=True)
            t1 = jnp.where(rank1 == float(a), count, t1)
        r2_ref[h] = rank2
        t1_ref[h] = t1
        a1_ref[h] = jnp.exp(s1 - v1[0:1]) / z
        a2_ref[h] = jnp.exp(s2 - v2[0:1])


def _peer_select(h2, wq, subkeys):
    t, d = h2.shape
    tm = min(t, 512)
    shape = jax.ShapeDtypeStruct((PEER_HEADS, PEER_NKEYS, t), F32)
    ospec = pl.BlockSpec((PEER_HEADS, PEER_NKEYS, tm), lambda i: (0, 0, i))
    return pl.pallas_call(
        _peer_select_kernel,
        grid=(t // tm,),
        in_specs=[pl.BlockSpec((tm, d), lambda i: (i, 0)),
                  pl.BlockSpec(wq.shape, lambda i: (0, 0)),
                  pl.BlockSpec(subkeys.shape, lambda i: (0, 0, 0, 0))],
        out_specs=[ospec] * 4,
        out_shape=[shape] * 4,
        compiler_params=_params("arbitrary"),
    )(h2, wq, subkeys)


def _gelu_tanh(x):
    return x * (0.5 * (1.0 + jnp.tanh(0.7978845608028654 * (x + 0.044715 * (x * x * x)))))


def _peer_apply_kernel(h2_ref, u_ref, vt_ref, r2_ref, t1_ref, a1_ref, a2_ref, x1_ref, g2_ref, fn_ref,
                       y_ref, acc_ref, act_ref, gw_ref, *, ne1):
    nb, lb, d = x1_ref.shape
    tm = nb * lb
    nk = PEER_NKEYS
    lanes = 128
    j = pl.program_id(2)

    @pl.when(j == 0)
    def _():
        acc_ref[...] = jnp.zeros_like(acc_ref)

    act_ref[...] = _gelu_tanh(lax.dot_general(u_ref[...], h2_ref[...], (((1,), (1,)), ((), ())),
                                              preferred_element_type=F32))

    e1_rows = pl.ds(pl.multiple_of(j * ne1, ne1), ne1)
    for tc in range(tm // lanes):
        ls = slice(tc * lanes, (tc + 1) * lanes)
        for e in range(ne1):
            wsum = jnp.zeros((nk, lanes), F32)
            for h in range(PEER_HEADS):
                t1 = t1_ref[h, e1_rows, ls][e:e + 1]
                a1 = a1_ref[h, e1_rows, ls][e:e + 1]
                wsum = wsum + jnp.where(r2_ref[h, :, ls] < t1, a1 * a2_ref[h, :, ls], 0.0)
            rs = slice(e * nk, (e + 1) * nk)
            gw_ref[rs, ls] = (wsum * act_ref[rs, ls]).astype(BF16)
    acc_ref[...] += jnp.dot(vt_ref[...], gw_ref[...], preferred_element_type=F32)

    @pl.when(j == pl.num_programs(2) - 1)
    def _():
        ff = acc_ref[...].T.reshape(nb, lb, d)
        y_ref[...] = _rms(x1_ref[...] + g2_ref[...] * ff) * fn_ref[...]


def _peer_apply(h2, u_bf16, vt_bf16, sel, x1, mod3, final_g):
    n, l, d = x1.shape
    nb, lb = _token_blocks(n, l)
    tm = nb * lb
    tpb = l // lb
    ne1 = 8
    te = ne1 * PEER_NKEYS
    selspec = pl.BlockSpec((PEER_HEADS, PEER_NKEYS, tm), lambda i, t, j: (0, 0, i * tpb + t))
    tok = pl.BlockSpec((nb, lb, d), lambda i, t, j: (i, t, 0))
    return pl.pallas_call(
        functools.partial(_peer_apply_kernel, ne1=ne1),
        grid=(n // nb, tpb, PEER_NKEYS // ne1),
        in_specs=[pl.BlockSpec((tm, d), lambda i, t, j: (i * tpb + t, 0)),
                  pl.BlockSpec((te, d), lambda i, t, j: (j, 0)),
                  pl.BlockSpec((d, te), lambda i, t, j: (0, j)),
                  selspec, selspec, selspec, selspec,
                  tok,
                  pl.BlockSpec((nb, 1, d), lambda i, t, j: (i, 0, 5)),
                  pl.BlockSpec((1, d), lambda i, t, j: (0, 0))],
        out_specs=tok,
        out_shape=jax.ShapeDtypeStruct((n, l, d), F32),
        scratch_shapes=[pltpu.VMEM((d, tm), F32), pltpu.VMEM((te, tm), F32), pltpu.VMEM((te, tm), BF16)],
        compiler_params=_params("arbitrary", "arbitrary", "arbitrary"),
    )(h2, u_bf16, vt_bf16, *sel, x1, mod3, final_g.reshape(1, d))


def _layer(x, mod, w, caches, s0, layer, final_g):
    n, l, d = x.shape
    mod3 = mod.reshape(n, 1, N_MOD * d)
    proj = _inproj(x, mod3, w["norm1_g"], w["w_in"])
    att_o, att_lse, bufs = [], [], []
    for g, (window, _) in enumerate(ATT_GROUPS):
        k0 = (COL_K + g) * ATT_GROUP_WIDTH
        v0 = (COL_V + g) * ATT_GROUP_WIDTH
        if caches is None:
            o, lse = _attn_prompt(proj, g)
            keep = min(window, l)
            kb = proj[:, l - keep:, k0:k0 + ATT_GROUP_WIDTH]
            vb = proj[:, l - keep:, v0:v0 + ATT_GROUP_WIDTH]
        else:
            ck, cv = caches[2 * g], caches[2 * g + 1]
            o, lse, kb, vb = _attn_sample(proj, ck.reshape(ck.shape[0], ck.shape[1], ATT_GROUP_WIDTH),
                                          cv.reshape(cv.shape[0], cv.shape[1], ATT_GROUP_WIDTH), g)
        att_o.append(o)
        att_lse.append(lse)
        bufs += [kb.reshape(n, kb.shape[1], ATT_HEADS_PER_GROUP, ATT_HEAD_DIM),
                 vb.reshape(n, vb.shape[1], ATT_HEADS_PER_GROUP, ATT_HEAD_DIM)]
    o_h, s_fin = _hgrn(proj, w["hg_lower_bound"], w["hg_norm_g"], s0, layer)
    x1, h2 = _merge(x, att_o, att_lse, o_h, proj, mod3, w["norm2_g"], w["w_att_branch"], w["w_hg_branch"], w["w_out"])
    sel = _peer_select(h2, w["peer_wq"], w["peer_subkeys"])
    y = _peer_apply(h2, w["peer_u"], w["peer_vt"], sel, x1, mod3, final_g)
    return y, bufs, s_fin


def kernel(x_prompt, x_sample, cache_k0, cache_v0, cache_k1, cache_v1, cache_k2, cache_v2, state_hgrn, c_prompt, c_sample, ada_w, ada_b, norm1_g, w_in, w_att_branch, w_hg_branch, w_out, hg_lower_bound, hg_norm_g, norm2_g, peer_wq, peer_subkeys, peer_u, peer_v, final_norm_g):
    depth = ada_w.shape[0]
    assert depth == 1, "the final RMSNorm is fused into the last PEER kernel of a single layer"
    n_p = x_prompt.shape[0]
    layer = 0
    w = {
        "norm1_g": norm1_g[layer],
        "w_in": jnp.concatenate([w_in[layer][:, ORIG_GATE_START:], w_in[layer][:, :ORIG_GATE_START]], axis=1).astype(BF16),
        "w_att_branch": w_att_branch[layer].astype(BF16),
        "w_hg_branch": w_hg_branch[layer].astype(BF16),
        "w_out": w_out[layer].astype(BF16),
        "hg_lower_bound": hg_lower_bound,
        "hg_norm_g": hg_norm_g[layer],
        "norm2_g": norm2_g[layer],
        "peer_wq": peer_wq[layer].astype(BF16),
        "peer_subkeys": peer_subkeys[layer].astype(BF16),
        "peer_u": peer_u[layer].astype(BF16),
        "peer_vt": peer_v[layer].astype(BF16).T,
    }
    mod = _adaln(jnp.concatenate([c_prompt, c_sample], axis=0), ada_w[layer], ada_b[layer])
    s0_p = jnp.zeros((n_p,) + state_hgrn.shape[2:], F32)
    y_p, bufs_p, st_p = _layer(x_prompt, mod[:n_p], w, None, s0_p, layer, final_norm_g)
    caches = (cache_k0[layer], cache_v0[layer], cache_k1[layer], cache_v1[layer], cache_k2[layer], cache_v2[layer])
    y_s, bufs_s, st_s = _layer(x_sample, mod[n_p:], w, caches, state_hgrn[layer], layer, final_norm_g)
    return (y_p, y_s, *[b[None] for b in bufs_p], st_p[None], *[b[None] for b in bufs_s], st_s[None])
```

```python
import functools

import jax
import jax.numpy as jnp
from jax import lax
from jax.experimental import pallas as pl
from jax.experimental.pallas import tpu as pltpu

F32 = jnp.float32
BF16 = jnp.bfloat16
NEG_INF = float("-inf")
LANES = 128
SUBLANES = 8
BF16_ROWS = 16

NORM_EPS = 1e-6
N_MOD = 6

ATT_HEAD_DIM = 64
ATT_GROUPS = ((128, 1), (512, 4), (2048, 16))
ATT_HEADS_PER_GROUP = 4
ATT_HEADS = len(ATT_GROUPS) * ATT_HEADS_PER_GROUP
ATT_GROUP_WIDTH = ATT_HEADS_PER_GROUP * ATT_HEAD_DIM
ATT_BLOCK = 128
ALIBI_MAX_EXP = 8.0

HG_HEADS = 4
HG_DK = 128
HG_DV = 128
HG_CHUNK = 64
HG_SUB = SUBLANES
HG_HEADS_PER_STEP = 2

PEER_HEADS = 8
PEER_NKEYS = 128
PEER_DHALF = 128
PEER_TOPK = 16
PEER_E1_PER_STEP = 8

COL_ZA, COL_ZH = 0, 1
COL_Q, COL_K, COL_V = 8, 11, 14
COL_ZF, COL_ZI, COL_ZQ, COL_ZG = 17, 19, 21, 23
ORIG_GATE_START = 4352

VMEM_LIMIT = 48 * 1024 * 1024


def _bdot(a, b):
    return jnp.dot(a.astype(BF16), b.astype(BF16), preferred_element_type=F32)


def _bdot_nt(a, b):
    return lax.dot_general(a.astype(BF16), b.astype(BF16), (((1,), (1,)), ((), ())),
                           preferred_element_type=F32)


def _log2(k):
    assert k > 0 and k & (k - 1) == 0, k
    return k.bit_length() - 1


def _div(x, k):
    return x >> _log2(k)


def _mod(x, k):
    return x & ((1 << _log2(k)) - 1)


def _rms(x):
    return x * lax.rsqrt(jnp.mean(x * x, axis=-1, keepdims=True) + NORM_EPS)


def _pad_rows(a, rows):
    if a.shape[0] == rows:
        return a
    return jnp.concatenate([a, jnp.zeros((rows - a.shape[0], a.shape[1]), a.dtype)], axis=0)


def _params(*sem):
    return pltpu.CompilerParams(dimension_semantics=sem, vmem_limit_bytes=VMEM_LIMIT)


def _adaln_kernel(c_ref, w_ref, b_ref, o_ref):
    c = c_ref[...]
    o_ref[...] = _bdot(c * jax.nn.sigmoid(c), w_ref[...]) + b_ref[...]


def _adaln(c, ada_w, ada_b):
    n, d = c.shape
    cols = ada_w.shape[1]
    tn = 1536
    return pl.pallas_call(
        _adaln_kernel,
        grid=(cols // tn,),
        in_specs=[pl.BlockSpec((n, d), lambda j: (0, 0)),
                  pl.BlockSpec((d, tn), lambda j: (0, j)),
                  pl.BlockSpec((1, tn), lambda j: (0, j))],
        out_specs=pl.BlockSpec((n, tn), lambda j: (0, j)),
        out_shape=jax.ShapeDtypeStruct((n, cols), F32),
        compiler_params=_params("arbitrary"),
        name="adaln",
    )(c, ada_w, ada_b.reshape(1, cols))


def _inproj_kernel(x_ref, sh_ref, sc_ref, g_ref, w_ref, o_ref, h_ref):
    nb, lb, d = x_ref.shape

    @pl.when(pl.program_id(2) == 0)
    def _():
        h = _rms(x_ref[...]) * g_ref[...] * (1.0 + sc_ref[...]) + sh_ref[...]
        h_ref[...] = h.reshape(nb * lb, d).astype(BF16)

    o = jnp.dot(h_ref[...], w_ref[...], preferred_element_type=F32)
    o_ref[...] = o.reshape(nb, lb, o.shape[-1])


def _token_blocks(n, l):
    if l >= 512:
        return 1, 512
    return min(n, 512 // l), l


def _inproj(x, mod3, norm_g, w_bf16):
    n, l, d = x.shape
    cols = w_bf16.shape[1]
    nb, lb = _token_blocks(n, l)
    tn = 1280
    return pl.pallas_call(
        _inproj_kernel,
        grid=(n // nb, l // lb, cols // tn),
        in_specs=[pl.BlockSpec((nb, lb, d), lambda i, t, j: (i, t, 0)),
                  pl.BlockSpec((nb, 1, d), lambda i, t, j: (i, 0, 0)),
                  pl.BlockSpec((nb, 1, d), lambda i, t, j: (i, 0, 1)),
                  pl.BlockSpec((1, d), lambda i, t, j: (0, 0)),
                  pl.BlockSpec((d, tn), lambda i, t, j: (0, j))],
        out_specs=pl.BlockSpec((nb, lb, tn), lambda i, t, j: (i, t, j)),
        out_shape=jax.ShapeDtypeStruct((n, l, cols), F32),
        scratch_shapes=[pltpu.VMEM((nb * lb, d), BF16)],
        compiler_params=_params("arbitrary", "arbitrary", "arbitrary"),
        name="inproj",
    )(x, mod3, mod3, norm_g.reshape(1, d), w_bf16)


def _alibi_slope(head):
    return 2.0 ** (-ALIBI_MAX_EXP * (head + 1) / ATT_HEADS)


def _attn_prompt_kernel(q_ref, kp_ref, kc_ref, vp_ref, vc_ref, o_ref, lse_ref, *, dil, n_taps, group):
    blk = ATT_BLOCK
    heads = LANES // ATT_HEAD_DIM
    n = pl.program_id(1)
    pair = pl.program_id(2)
    i = lax.broadcasted_iota(jnp.int32, (blk, 2 * blk), 0)
    j = lax.broadcasted_iota(jnp.int32, (blk, 2 * blk), 1)
    delta = i + blk - j
    valid = (delta >= 0) & (delta <= n_taps) & ((j >= blk) | (n > 0))
    dist = (delta * dil).astype(F32)
    for r in range(dil):
        rows = pl.ds(r, blk, stride=dil) if dil > 1 else slice(None)
        q = q_ref[0, rows, :]
        kk = jnp.concatenate([kp_ref[0, rows, :], kc_ref[0, rows, :]], axis=0)
        vv = jnp.concatenate([vp_ref[0, rows, :], vc_ref[0, rows, :]], axis=0)
        outs, lses = [], []
        for h in range(heads):
            cs = slice(h * ATT_HEAD_DIM, (h + 1) * ATT_HEAD_DIM)
            slope = _alibi_slope(group * ATT_HEADS_PER_GROUP + h)
            for other in range(1, ATT_HEADS_PER_GROUP // heads):
                slope = jnp.where(pair == other, _alibi_slope(group * ATT_HEADS_PER_GROUP + other * heads + h), slope)
            s = _bdot_nt(q[:, cs], kk[:, cs]) * (ATT_HEAD_DIM ** -0.5) - slope * dist
            s = jnp.where(valid, s, NEG_INF)
            m = jnp.max(s, axis=-1, keepdims=True)
            p = jnp.exp(s - m)
            l = jnp.sum(p, axis=-1, keepdims=True)
            outs.append(_bdot(p, vv[:, cs]) / l)
            lses.append(jnp.broadcast_to(m + jnp.log(l), (blk, ATT_HEAD_DIM)))
        o_ref[0, rows, :] = jnp.concatenate(outs, axis=1)
        lse_ref[0, rows, :] = jnp.concatenate(lses, axis=1)


def _attn_prompt(proj, group):
    window, dil = ATT_GROUPS[group]
    b, s, _ = proj.shape
    gw = ATT_GROUP_WIDTH
    span = dil * ATT_BLOCK
    assert s % span == 0

    pairs = gw // LANES

    def spec(col, prev):
        if prev:
            return pl.BlockSpec((1, span, LANES), lambda bb, n, hp: (bb, jnp.maximum(n - 1, 0), (col + group) * pairs + hp))
        return pl.BlockSpec((1, span, LANES), lambda bb, n, hp: (bb, n, (col + group) * pairs + hp))

    out_spec = pl.BlockSpec((1, span, LANES), lambda bb, n, hp: (bb, n, hp))
    return pl.pallas_call(
        functools.partial(_attn_prompt_kernel, dil=dil, n_taps=window // dil, group=group),
        grid=(b, s // span, pairs),
        in_specs=[spec(COL_Q, False), spec(COL_K, True), spec(COL_K, False), spec(COL_V, True), spec(COL_V, False)],
        out_specs=[out_spec, out_spec],
        out_shape=[jax.ShapeDtypeStruct((b, s, gw), F32)] * 2,
        compiler_params=_params("arbitrary", "arbitrary", "arbitrary"),
        name=f"attn_prompt_g{group}",
    )(proj, proj, proj, proj, proj)


def _attn_sample_kernel(q_ref, kn_ref, vn_ref, ck_ref, cv_ref, o_ref, lse_ref, nk_ref, nv_ref, *, window, dil, group):
    t_new = q_ref.shape[1]
    w = ck_ref.shape[2]
    gw = ATT_GROUP_WIDTH
    nh = ATT_HEADS_PER_GROUP
    rows = nh * t_new
    q, kn, vn = q_ref[0], kn_ref[0], vn_ref[0]
    ck, cv = ck_ref[0], cv_ref[0]

    r = lax.broadcasted_iota(jnp.int32, (rows, gw), 0)
    c = lax.broadcasted_iota(jnp.int32, (rows, gw), 1)
    head_mask = _div(r, t_new) == _div(c, ATT_HEAD_DIM)
    qbd = jnp.where(head_mask, jnp.concatenate([q] * nh, axis=0), 0.0)
    s_c = _bdot(qbd, ck) * (ATT_HEAD_DIM ** -0.5)
    s_n = _bdot_nt(qbd, kn) * (ATT_HEAD_DIM ** -0.5)

    hrow = _div(lax.broadcasted_iota(jnp.int32, (rows, 1), 0), t_new)
    slope = jnp.zeros((rows, 1), F32)
    for h in range(nh):
        slope = jnp.where(hrow == h, _alibi_slope(group * nh + h), slope)

    def masked(s, first_pos):
        tq = _mod(lax.broadcasted_iota(jnp.int32, s.shape, 0), t_new)
        dist = w + tq - (first_pos + lax.broadcasted_iota(jnp.int32, s.shape, 1))
        valid = (dist >= 0) & (_mod(dist, dil) == 0) & (dist <= window)
        return jnp.where(valid, s - slope * dist.astype(F32), NEG_INF)

    s_c = masked(s_c, 0)
    s_n = masked(s_n, w)
    m = jnp.maximum(jnp.max(s_c, axis=-1, keepdims=True), jnp.max(s_n, axis=-1, keepdims=True))
    p_c = jnp.exp(s_c - m)
    p_n = jnp.exp(s_n - m)
    l = jnp.sum(p_c, axis=-1, keepdims=True) + jnp.sum(p_n, axis=-1, keepdims=True)
    o_full = jnp.where(head_mask, (_bdot_nt(p_c, cv) + _bdot(p_n, vn)) / l, 0.0)
    lse_full = jnp.where(head_mask, m + jnp.log(l), 0.0)
    o, lse = o_full[0:t_new], lse_full[0:t_new]
    for h in range(1, nh):
        o = o + o_full[h * t_new:(h + 1) * t_new]
        lse = lse + lse_full[h * t_new:(h + 1) * t_new]
    o_ref[0] = o
    lse_ref[0] = lse

    lane = lax.broadcasted_iota(jnp.int32, (gw, LANES), 1)
    for new, old, out_ref in ((kn, ck, nk_ref), (vn, cv, nv_ref)):
        shifted = pltpu.roll(old, w - t_new, 1)
        new_t = pltpu.roll(_pad_rows(new, LANES).T, LANES - t_new, 1)
        if w > LANES:
            out_ref[0, :, 0:w - LANES] = shifted[:, 0:w - LANES]
        out_ref[0, :, w - LANES:w] = jnp.where(lane >= LANES - t_new, new_t, shifted[:, w - LANES:w])


def _attn_sample(proj, cache_k, cache_v, group):
    window, dil = ATT_GROUPS[group]
    bd, t_new, _ = proj.shape
    w = cache_k.shape[2]
    gw = ATT_GROUP_WIDTH
    assert w >= window and w % LANES == 0 and t_new % SUBLANES == 0 and t_new < LANES

    def pspec(col):
        return pl.BlockSpec((1, t_new, gw), lambda bb: (bb, 0, col + group))

    cspec = pl.BlockSpec((1, gw, w), lambda bb: (bb, 0, 0))
    ospec = pl.BlockSpec((1, t_new, gw), lambda bb: (bb, 0, 0))
    return pl.pallas_call(
        functools.partial(_attn_sample_kernel, window=window, dil=dil, group=group),
        grid=(bd,),
        in_specs=[pspec(COL_Q), pspec(COL_K), pspec(COL_V), cspec, cspec],
        out_specs=[ospec, ospec, cspec, cspec],
        out_shape=[jax.ShapeDtypeStruct((bd, t_new, gw), F32)] * 2 + [jax.ShapeDtypeStruct((bd, gw, w), F32)] * 2,
        compiler_params=_params("arbitrary"),
        name=f"attn_sample_g{group}",
    )(proj, proj, proj, cache_k, cache_v)


def _hgrn_levels(chunk):
    levels, m = [], HG_SUB
    while m < chunk:
        levels.append(m)
        m *= 2
    return tuple(levels)


def _hgrn_kernel(zf_ref, zi_ref, zq_ref, zg_ref, lbraw_ref, gn_ref, s0_ref, o_ref, sfin_ref, st_ref, *, layer):
    chunk = zf_ref.shape[1]
    levels = _hgrn_levels(chunk)
    cidx = pl.program_id(2)
    dk, dv = HG_DK, HG_DV

    @pl.when(cidx == 0)
    def _():
        for hh in range(HG_HEADS_PER_STEP):
            st_ref[hh] = s0_ref[0, hh].T

    raw = lbraw_ref[...]
    e = jnp.exp(raw - jnp.max(raw, axis=0, keepdims=True))
    lb_all = jnp.sum(e[0:layer + 1], axis=0, keepdims=True) / jnp.sum(e, axis=0, keepdims=True)

    rr = lax.broadcasted_iota(jnp.int32, (chunk, chunk), 0)
    cc = lax.broadcasted_iota(jnp.int32, (chunk, chunk), 1)
    sels = [cc <= rr] + [cc <= _div(rr, 2 * m) * (2 * m) + m - 1 for m in levels]
    sel = jnp.concatenate([jnp.where(sm, 1.0, 0.0) for sm in sels], axis=0).astype(BF16)

    trow = lax.broadcasted_iota(jnp.int32, (chunk, dk), 0)
    for hh in range(HG_HEADS_PER_STEP):
        cs = slice(hh * dk, (hh + 1) * dk)
        lb = lb_all[:, cs]
        f = lb + (1.0 - lb) * jax.nn.sigmoid(zf_ref[0][:, cs])
        lf = jnp.log(f)
        kh = 1.0 - f
        q = zq_ref[0][:, cs]
        v = zi_ref[0][:, cs]

        hi = lf.astype(BF16)
        r1 = lf - hi.astype(F32)
        mid = r1.astype(BF16)
        lo = (r1 - mid.astype(F32)).astype(BF16)
        bb = jnp.dot(sel, jnp.concatenate([hi, mid, lo], axis=1), preferred_element_type=F32)
        bb = bb[:, 0:dk] + bb[:, dk:2 * dk] + bb[:, 2 * dk:3 * dk]
        b = bb[0:chunk]

        st = st_ref[hh]
        o = _bdot_nt(q * jnp.exp(b), st)

        for off in range(HG_SUB):
            if off == 0:
                a = jnp.sum(q * kh, axis=-1, keepdims=True)
                o = o + a * v
            else:
                in_block = _mod(trow, HG_SUB) >= off
                dec = jnp.exp(jnp.where(in_block, b - pltpu.roll(b, off, 0), NEG_INF))
                a = jnp.sum(q * pltpu.roll(kh, off, 0) * dec, axis=-1, keepdims=True)
                o = o + a * pltpu.roll(v, off, 0)

        if levels:
            ti = lax.broadcasted_iota(jnp.int32, (chunk, chunk), 0)
            si = lax.broadcasted_iota(jnp.int32, (chunk, chunk), 1)
            amat = jnp.zeros((chunk, chunk), F32)
            for li, m in enumerate(levels):
                bm = bb[(li + 1) * chunk:(li + 2) * chunk]
                second = _mod(_div(trow, m), 2) == 1
                qs = q * jnp.exp(jnp.where(second, b - bm, NEG_INF))
                ks = kh * jnp.exp(jnp.where(second, NEG_INF, bm - b))
                amat = amat + jnp.where(_div(ti, 2 * m) == _div(si, 2 * m), _bdot_nt(qs, ks), 0.0)
            o = o + _bdot(amat, v)

        b_end = b[chunk - 1:chunk]
        kd = kh * jnp.exp(b_end - b)
        vt = _pad_rows(v, dv).T
        st_ref[hh] = st * jnp.exp(b_end) + _bdot(vt, _pad_rows(kd, dv))

        zg = zg_ref[0][:, cs]
        o_ref[0, :, cs] = _rms(o) * gn_ref[...] * (zg * jax.nn.sigmoid(zg))

    @pl.when(cidx == pl.num_programs(2) - 1)
    def _():
        for hh in range(HG_HEADS_PER_STEP):
            sfin_ref[0, hh] = st_ref[hh].T


def _hgrn(proj, hg_lower_bound, hg_norm_g, s0, layer):
    b, l, _ = proj.shape
    assert HG_DK == HG_DV and HG_DV == LANES
    chunk = min(HG_CHUNK, l)
    assert l % chunk == 0 and chunk % HG_SUB == 0
    hps = HG_HEADS_PER_STEP
    wblk = hps * HG_DK

    def pspec(col):
        return pl.BlockSpec((1, chunk, wblk), lambda bb, hp, c: (bb, c, col + hp))

    sspec = pl.BlockSpec((1, hps, HG_DK, HG_DV), lambda bb, hp, c: (bb, hp, 0, 0))
    return pl.pallas_call(
        functools.partial(_hgrn_kernel, layer=layer),
        grid=(b, HG_HEADS // hps, l // chunk),
        in_specs=[pspec(COL_ZF), pspec(COL_ZI), pspec(COL_ZQ), pspec(COL_ZG),
                  pl.BlockSpec((hg_lower_bound.shape[0], wblk), lambda bb, hp, c: (0, hp)),
                  pl.BlockSpec((1, HG_DV), lambda bb, hp, c: (0, 0)),
                  sspec],
        out_specs=[pl.BlockSpec((1, chunk, wblk), lambda bb, hp, c: (bb, c, hp)), sspec],
        out_shape=[jax.ShapeDtypeStruct((b, l, HG_HEADS * HG_DV), F32),
                   jax.ShapeDtypeStruct(s0.shape, F32)],
        scratch_shapes=[pltpu.VMEM((hps, HG_DV, HG_DK), F32)],
        compiler_params=_params("arbitrary", "arbitrary", "arbitrary"),
        name="hgrn2",
    )(proj, proj, proj, proj, hg_lower_bound, hg_norm_g.reshape(1, HG_DV), s0)


def _merge_kernel(x_ref, o0_ref, l0_ref, o1_ref, l1_ref, o2_ref, l2_ref, oh_ref, za_ref, zh_ref,
                  g1_ref, sh2_ref, sc2_ref, n2_ref, wa_ref, wh_ref, wo_ref, x1_ref, h2_ref):
    nb, lb, d = x_ref.shape
    tm = nb * lb

    def flat(ref):
        return ref[...].reshape(tm, ref.shape[-1])

    l0, l1, l2 = flat(l0_ref), flat(l1_ref), flat(l2_ref)
    m = jnp.maximum(jnp.maximum(l0, l1), l2)
    e0, e1, e2 = jnp.exp(l0 - m), jnp.exp(l1 - m), jnp.exp(l2 - m)
    att = (e0 * flat(o0_ref) + e1 * flat(o1_ref) + e2 * flat(o2_ref)) / (e0 + e1 + e2)
    merged = (jax.nn.sigmoid(flat(za_ref)) * _bdot(att, wa_ref[...])
              + jax.nn.sigmoid(flat(zh_ref)) * _bdot(flat(oh_ref), wh_ref[...]))
    mix = _bdot(merged, wo_ref[...])
    x1 = x_ref[...] + g1_ref[...] * mix.reshape(nb, lb, d)
    x1_ref[...] = x1
    h2 = _rms(x1) * n2_ref[...] * (1.0 + sc2_ref[...]) + sh2_ref[...]
    h2_ref[...] = h2.reshape(tm, d).astype(BF16)


def _merge(x, att_o, att_lse, o_h, proj, mod3, norm2_g, wa, wh, wo):
    n, l, d = x.shape
    nb, lb = _token_blocks(n, l)
    gw = ATT_GROUP_WIDTH
    tpb = l // lb

    def tok(width, col=0):
        return pl.BlockSpec((nb, lb, width), lambda i, t: (i, t, col))

    def modspec(col):
        return pl.BlockSpec((nb, 1, d), lambda i, t: (i, 0, col))

    def full(a):
        return pl.BlockSpec(a.shape, lambda i, t: (0, 0))

    att_specs, att_args = [], []
    for o, lse in zip(att_o, att_lse):
        att_specs += [tok(gw), tok(gw)]
        att_args += [o, lse]
    return pl.pallas_call(
        _merge_kernel,
        grid=(n // nb, tpb),
        in_specs=[tok(d)] + att_specs + [tok(o_h.shape[-1]), tok(d, COL_ZA), tok(d, COL_ZH),
                                         modspec(2), modspec(3), modspec(4),
                                         pl.BlockSpec((1, d), lambda i, t: (0, 0)), full(wa), full(wh), full(wo)],
        out_specs=[tok(d), pl.BlockSpec((nb * lb, d), lambda i, t: (i * tpb + t, 0))],
        out_shape=[jax.ShapeDtypeStruct((n, l, d), F32), jax.ShapeDtypeStruct((n * l, d), BF16)],
        compiler_params=_params("arbitrary", "arbitrary"),
        name="merge",
    )(x, *att_args, o_h, proj, proj, mod3, mod3, mod3, norm2_g.reshape(1, d), wa, wh, wo)


def _top16(arrays):
    krow = lax.broadcasted_iota(jnp.int32, (PEER_TOPK, LANES), 0)

    def body(i, carry):
        out = []
        for s, vals, idxs in carry:
            rows = s.shape[0]
            row = lax.broadcasted_iota(jnp.int32, s.shape, 0)
            m = jnp.max(s, axis=0, keepdims=True)
            idx = jnp.min(jnp.where(s == m, row, rows), axis=0, keepdims=True)
            s = jnp.where(row == idx, NEG_INF, s)
            vals = jnp.where(krow == i, m, vals)
            idxs = jnp.where(krow == i, idx, idxs)
            out.append((s, vals, idxs))
        return tuple(out)

    init = tuple((s, jnp.zeros((PEER_TOPK, LANES), F32), jnp.zeros((PEER_TOPK, LANES), jnp.int32)) for s in arrays)
    return [(vals, idxs) for _, vals, idxs in lax.fori_loop(0, PEER_TOPK, body, init)]


def _peer_select_kernel(h2_ref, wq_ref, sk_ref, r2_ref, t1_ref, a1_ref, a2_ref, s_ref):
    tm = h2_ref.shape[0]
    kk = PEER_TOPK
    nk = PEER_NKEYS
    qry = jnp.dot(h2_ref[...], wq_ref[...], preferred_element_type=F32).astype(BF16)
    for h in range(PEER_HEADS):
        for p in range(2):
            c0 = (2 * h + p) * PEER_DHALF
            st = _bdot_nt(sk_ref[h, p], qry[:, c0:c0 + PEER_DHALF])
            for c in range(tm // LANES):
                s_ref[2 * h + p, c] = st[:, c * LANES:(c + 1) * LANES]

    row = lax.broadcasted_iota(jnp.int32, (nk, LANES), 0)
    krow = lax.broadcasted_iota(jnp.int32, (kk, LANES), 0)

    def candidates(v1, v2):
        return jnp.concatenate([v1[0:1] + v2] + [v1[a:a + 1] + v2[0:SUBLANES] for a in range(1, SUBLANES)]
                               + [v1[SUBLANES:kk] + v2[0:1]], axis=0)

    def finish(h, c, v1, i1, v2, i2, sf, ci):
        first = jnp.where(ci < kk, 0, jnp.where(ci < kk + 7 * SUBLANES, 1 + _div(ci - kk, SUBLANES),
                                                ci - (kk + 7 * SUBLANES) + SUBLANES))
        z = jnp.sum(jnp.exp(sf - sf[0:1]), axis=0, keepdims=True)
        count = jnp.zeros((kk, LANES), F32)
        for k in range(kk):
            count = count + jnp.where(krow == first[k:k + 1], 1.0, 0.0)
        t1 = jnp.zeros((nk, LANES), F32)
        r2 = jnp.full((nk, LANES), float(kk), F32)
        for a in range(kk):
            t1 = jnp.where(row == i1[a:a + 1], count[a:a + 1], t1)
            r2 = jnp.where(row == i2[a:a + 1], float(a), r2)
        r2_ref[h, c] = r2
        t1_ref[h, c] = t1
        a1_ref[h, c] = jnp.exp(s_ref[2 * h, c] - v1[0:1]) / z
        a2_ref[h, c] = jnp.exp(s_ref[2 * h + 1, c] - v2[0:1])

    def chunk_body(c, carry):
        for h0 in range(0, PEER_HEADS, 2):
            halves = []
            for h in (h0, h0 + 1):
                (v1, i1), (v2, i2) = _top16([s_ref[2 * h, c], s_ref[2 * h + 1, c]])
                halves.append((v1, i1, v2, i2))
            picks = _top16([candidates(v1, v2) for v1, _, v2, _ in halves])
            for h, half, (sf, ci) in zip((h0, h0 + 1), halves, picks):
                finish(h, c, *half, sf, ci)
        return carry

    lax.fori_loop(0, tm // LANES, chunk_body, 0)


def _peer_select(h2, wq, subkeys):
    t, d = h2.shape
    tm = min(t, 512)
    assert t % tm == 0 and tm % LANES == 0
    nch = tm // LANES
    ospec = pl.BlockSpec((PEER_HEADS, nch, PEER_NKEYS, LANES), lambda i: (0, i, 0, 0))

    def shape(dtype):
        return jax.ShapeDtypeStruct((PEER_HEADS, t // LANES, PEER_NKEYS, LANES), dtype)

    return pl.pallas_call(
        _peer_select_kernel,
        grid=(t // tm,),
        in_specs=[pl.BlockSpec((tm, d), lambda i: (i, 0)),
                  pl.BlockSpec(wq.shape, lambda i: (0, 0)),
                  pl.BlockSpec(subkeys.shape, lambda i: (0, 0, 0, 0))],
        out_specs=[ospec] * 4,
        out_shape=[shape(F32)] * 4,
        scratch_shapes=[pltpu.VMEM((2 * PEER_HEADS, nch, PEER_NKEYS, LANES), F32)],
        compiler_params=_params("arbitrary"),
        name="peer_select",
    )(h2, wq, subkeys)


def _gelu_tanh(x):
    return x * (0.5 * (1.0 + jnp.tanh(0.7978845608028654 * (x + 0.044715 * (x * x * x)))))


def _peer_apply_kernel(h2_ref, u_ref, vt_ref, r2_ref, t1_ref, a1_ref, a2_ref, x1_ref, g2_ref, fn_ref,
                       y_ref, acc_ref, act_ref, gw_ref, r2b_ref, a2b_ref):
    nb, lb, d = x1_ref.shape
    tm = nb * lb
    nch = tm // LANES
    nk = PEER_NKEYS
    ne1 = PEER_E1_PER_STEP
    groups = nk // BF16_ROWS
    j = pl.program_id(2)

    @pl.when(j == 0)
    def _():
        acc_ref[...] = jnp.zeros_like(acc_ref)
        for h in range(PEER_HEADS):
            for tc in range(tm // LANES):
                r2b_ref[h, tc] = r2_ref[h, tc].astype(BF16)
                a2b_ref[h, tc] = a2_ref[h, tc].astype(BF16)

    act_ref[...] = _gelu_tanh(lax.dot_general(u_ref[...], h2_ref[...], (((1,), (1,)), ((), ())),
                                              preferred_element_type=F32)).astype(BF16)

    def bcast(tile, e):
        return jnp.broadcast_to(tile[e:e + 1], (BF16_ROWS, LANES)).astype(BF16)

    e1_rows = pl.ds(pl.multiple_of(j * ne1, ne1), ne1)
    zero = jnp.zeros((BF16_ROWS, LANES), BF16)
    for tc in range(nch):
        ls = slice(tc * LANES, (tc + 1) * LANES)
        for e in range(ne1):
            wsum = [zero] * groups
            for h in range(PEER_HEADS):
                t1 = bcast(t1_ref[h, tc, e1_rows, :], e)
                a1 = bcast(a1_ref[h, tc, e1_rows, :], e)
                for g in range(groups):
                    gs = slice(g * BF16_ROWS, (g + 1) * BF16_ROWS)
                    wsum[g] = wsum[g] + jnp.where(r2b_ref[h, tc, gs, :] < t1, a1 * a2b_ref[h, tc, gs, :], zero)
            for g in range(groups):
                rs = slice(e * nk + g * BF16_ROWS, e * nk + (g + 1) * BF16_ROWS)
                gw_ref[rs, ls] = wsum[g] * act_ref[rs, ls]
    acc_ref[...] += jnp.dot(vt_ref[...], gw_ref[...], preferred_element_type=F32)

    @pl.when(j == pl.num_programs(2) - 1)
    def _():
        ff = acc_ref[...].T.reshape(nb, lb, d)
        y_ref[...] = _rms(x1_ref[...] + g2_ref[...] * ff) * fn_ref[...]


def _peer_apply(h2, u_bf16, vt_bf16, sel, x1, mod3, final_g):
    n, l, d = x1.shape
    nb, lb = _token_blocks(n, l)
    tm = nb * lb
    tpb = l // lb
    ne1 = PEER_E1_PER_STEP
    assert ne1 == SUBLANES and tm % LANES == 0
    te = ne1 * PEER_NKEYS
    nch = tm // LANES
    selspec = pl.BlockSpec((PEER_HEADS, nch, PEER_NKEYS, LANES), lambda i, t, j: (0, i * tpb + t, 0, 0))
    tok = pl.BlockSpec((nb, lb, d), lambda i, t, j: (i, t, 0))
    return pl.pallas_call(
        _peer_apply_kernel,
        grid=(n // nb, tpb, PEER_NKEYS // ne1),
        in_specs=[pl.BlockSpec((tm, d), lambda i, t, j: (i * tpb + t, 0)),
                  pl.BlockSpec((te, d), lambda i, t, j: (j, 0)),
                  pl.BlockSpec((d, te), lambda i, t, j: (0, j)),
                  selspec, selspec, selspec, selspec,
                  tok,
                  pl.BlockSpec((nb, 1, d), lambda i, t, j: (i, 0, 5)),
                  pl.BlockSpec((1, d), lambda i, t, j: (0, 0))],
        out_specs=tok,
        out_shape=jax.ShapeDtypeStruct((n, l, d), F32),
        scratch_shapes=[pltpu.VMEM((d, tm), F32),
                        pltpu.VMEM((te, tm), BF16), pltpu.VMEM((te, tm), BF16),
                        pltpu.VMEM((PEER_HEADS, nch, PEER_NKEYS, LANES), BF16),
                        pltpu.VMEM((PEER_HEADS, nch, PEER_NKEYS, LANES), BF16)],
        compiler_params=_params("arbitrary", "arbitrary", "arbitrary"),
        name="peer_apply",
    )(h2, u_bf16, vt_bf16, *sel, x1, mod3, final_g.reshape(1, d))


def _to_rows(buf):
    n, _, w = buf.shape
    return buf.reshape(n, ATT_HEADS_PER_GROUP, ATT_HEAD_DIM, w).transpose(0, 3, 1, 2)


def _to_cols(buf):
    n, w = buf.shape[0], buf.shape[1]
    return buf.transpose(0, 2, 3, 1).reshape(n, ATT_GROUP_WIDTH, w)


def _layer(x, mod, w, caches, s0, layer, final_g):
    n, l, d = x.shape
    mod3 = mod.reshape(n, 1, N_MOD * d)
    proj = _inproj(x, mod3, w["norm1_g"], w["w_in"])
    att_o, att_lse, bufs = [], [], []
    for g, (window, _) in enumerate(ATT_GROUPS):
        if caches is None:
            o, lse = _attn_prompt(proj, g)
            keep = min(window, l)
            k0 = (COL_K + g) * ATT_GROUP_WIDTH
            v0 = (COL_V + g) * ATT_GROUP_WIDTH
            kb = proj[:, l - keep:, k0:k0 + ATT_GROUP_WIDTH].reshape(n, keep, ATT_HEADS_PER_GROUP, ATT_HEAD_DIM)
            vb = proj[:, l - keep:, v0:v0 + ATT_GROUP_WIDTH].reshape(n, keep, ATT_HEADS_PER_GROUP, ATT_HEAD_DIM)
        else:
            o, lse, kb, vb = _attn_sample(proj, _to_cols(caches[2 * g]), _to_cols(caches[2 * g + 1]), g)
            kb, vb = _to_rows(kb), _to_rows(vb)
        att_o.append(o)
        att_lse.append(lse)
        bufs += [kb, vb]
    o_h, s_fin = _hgrn(proj, w["hg_lower_bound"], w["hg_norm_g"], s0, layer)
    x1, h2 = _merge(x, att_o, att_lse, o_h, proj, mod3, w["norm2_g"], w["w_att_branch"], w["w_hg_branch"], w["w_out"])
    sel = _peer_select(h2, w["peer_wq"], w["peer_subkeys"])
    y = _peer_apply(h2, w["peer_u"], w["peer_vt"], sel, x1, mod3, final_g)
    return y, bufs, s_fin


def kernel(x_prompt, x_sample, cache_k0, cache_v0, cache_k1, cache_v1, cache_k2, cache_v2, state_hgrn, c_prompt, c_sample, ada_w, ada_b, norm1_g, w_in, w_att_branch, w_hg_branch, w_out, hg_lower_bound, hg_norm_g, norm2_g, peer_wq, peer_subkeys, peer_u, peer_v, final_norm_g):
    depth = ada_w.shape[0]
    assert depth == 1, "the final RMSNorm is fused into the last PEER kernel of a single layer"
    n_p = x_prompt.shape[0]
    layer = 0
    w = {
        "norm1_g": norm1_g[layer],
        "w_in": jnp.concatenate([w_in[layer][:, ORIG_GATE_START:], w_in[layer][:, :ORIG_GATE_START]], axis=1).astype(BF16),
        "w_att_branch": w_att_branch[layer].astype(BF16),
        "w_hg_branch": w_hg_branch[layer].astype(BF16),
        "w_out": w_out[layer].astype(BF16),
        "hg_lower_bound": hg_lower_bound,
        "hg_norm_g": hg_norm_g[layer],
        "norm2_g": norm2_g[layer],
        "peer_wq": peer_wq[layer].astype(BF16),
        "peer_subkeys": peer_subkeys[layer].astype(BF16),
        "peer_u": peer_u[layer].astype(BF16),
        "peer_vt": peer_v[layer].astype(BF16).T,
    }
    mod = _adaln(jnp.concatenate([c_prompt, c_sample], axis=0), ada_w[layer], ada_b[layer])
    s0_p = jnp.zeros((n_p,) + state_hgrn.shape[2:], F32)
    y_p, bufs_p, st_p = _layer(x_prompt, mod[:n_p], w, None, s0_p, layer, final_norm_g)
    caches = (cache_k0[layer], cache_v0[layer], cache_k1[layer], cache_v1[layer], cache_k2[layer], cache_v2[layer])
    y_s, bufs_s, st_s = _layer(x_sample, mod[n_p:], w, caches, state_hgrn[layer], layer, final_norm_g)
    return (y_p, y_s, *[b[None] for b in bufs_p], st_p[None], *[b[None] for b in bufs_s], st_s[None])
```

```python
import functools

import jax
import jax.numpy as jnp
from jax import lax
from jax.experimental import pallas as pl
from jax.experimental.pallas import tpu as pltpu

F32 = jnp.float32
BF16 = jnp.bfloat16
NEG_INF = float("-inf")
LANES = 128
SUBLANES = 8
BF16_ROWS = 16

NORM_EPS = 1e-6
N_MOD = 6

ATT_HEAD_DIM = 64
ATT_GROUPS = ((128, 1), (512, 4), (2048, 16))
ATT_HEADS_PER_GROUP = 4
ATT_HEADS = len(ATT_GROUPS) * ATT_HEADS_PER_GROUP
ATT_GROUP_WIDTH = ATT_HEADS_PER_GROUP * ATT_HEAD_DIM
ATT_BLOCK = 128
ATT_TOKENS_PER_STEP = 512
ATT_BLOCKS_PER_TRIP = 4
ATT_SEQS_PER_STEP = (8, 4, 1)
ALIBI_MAX_EXP = 8.0

HG_HEADS = 4
HG_DK = 128
HG_DV = 128
HG_CHUNK = 64
HG_SUB = SUBLANES
HG_HEADS_PER_STEP = 2
HG_SEQS_PER_STEP = 8

PEER_HEADS = 8
PEER_NKEYS = 128
PEER_DHALF = 128
PEER_TOPK = 16
PEER_E1_PER_STEP = 8

COL_ZA, COL_ZH = 0, 1
COL_Q, COL_K, COL_V = 8, 11, 14
COL_ZF, COL_ZI, COL_ZQ, COL_ZG = 17, 19, 21, 23
ORIG_GATE_START = 4352

VMEM_LIMIT = 48 * 1024 * 1024


def _bdot(a, b):
    return jnp.dot(a.astype(BF16), b.astype(BF16), preferred_element_type=F32)


def _bdot_nt(a, b):
    return lax.dot_general(a.astype(BF16), b.astype(BF16), (((1,), (1,)), ((), ())),
                           preferred_element_type=F32)


def _log2(k):
    assert k > 0 and k & (k - 1) == 0, k
    return k.bit_length() - 1


def _div(x, k):
    return x >> _log2(k)


def _mod(x, k):
    return x & ((1 << _log2(k)) - 1)


def _rms(x):
    return x * lax.rsqrt(jnp.mean(x * x, axis=-1, keepdims=True) + NORM_EPS)


def _pad_rows(a, rows):
    if a.shape[0] == rows:
        return a
    return jnp.concatenate([a, jnp.zeros((rows - a.shape[0], a.shape[1]), a.dtype)], axis=0)


def _params(*sem):
    return pltpu.CompilerParams(dimension_semantics=sem, vmem_limit_bytes=VMEM_LIMIT)


def _adaln_kernel(c_ref, w_ref, b_ref, o_ref):
    c = c_ref[...]
    o_ref[...] = _bdot(c * jax.nn.sigmoid(c), w_ref[...]) + b_ref[...]


def _adaln(c, ada_w, ada_b):
    n, d = c.shape
    cols = ada_w.shape[1]
    tn = 1536
    return pl.pallas_call(
        _adaln_kernel,
        grid=(cols // tn,),
        in_specs=[pl.BlockSpec((n, d), lambda j: (0, 0)),
                  pl.BlockSpec((d, tn), lambda j: (0, j)),
                  pl.BlockSpec((1, tn), lambda j: (0, j))],
        out_specs=pl.BlockSpec((n, tn), lambda j: (0, j)),
        out_shape=jax.ShapeDtypeStruct((n, cols), F32),
        compiler_params=_params("arbitrary"),
        name="adaln",
    )(c, ada_w, ada_b.reshape(1, cols))


def _inproj_kernel(x_ref, sh_ref, sc_ref, g_ref, w_ref, o_ref, h_ref):
    nb, lb, d = x_ref.shape

    @pl.when(pl.program_id(2) == 0)
    def _():
        h = _rms(x_ref[...]) * g_ref[...] * (1.0 + sc_ref[...]) + sh_ref[...]
        h_ref[...] = h.reshape(nb * lb, d).astype(BF16)

    o = jnp.dot(h_ref[...], w_ref[...], preferred_element_type=F32)
    o_ref[...] = o.reshape(nb, lb, o.shape[-1])


def _token_blocks(n, l):
    if l >= 512:
        return 1, 512
    return min(n, 512 // l), l


def _inproj(x, mod3, norm_g, w_bf16):
    n, l, d = x.shape
    cols = w_bf16.shape[1]
    nb, lb = _token_blocks(n, l)
    tn = 1280
    return pl.pallas_call(
        _inproj_kernel,
        grid=(n // nb, l // lb, cols // tn),
        in_specs=[pl.BlockSpec((nb, lb, d), lambda i, t, j: (i, t, 0)),
                  pl.BlockSpec((nb, 1, d), lambda i, t, j: (i, 0, 0)),
                  pl.BlockSpec((nb, 1, d), lambda i, t, j: (i, 0, 1)),
                  pl.BlockSpec((1, d), lambda i, t, j: (0, 0)),
                  pl.BlockSpec((d, tn), lambda i, t, j: (0, j))],
        out_specs=pl.BlockSpec((nb, lb, tn), lambda i, t, j: (i, t, j)),
        out_shape=jax.ShapeDtypeStruct((n, l, cols), F32),
        scratch_shapes=[pltpu.VMEM((nb * lb, d), BF16)],
        compiler_params=_params("arbitrary", "arbitrary", "arbitrary"),
        name="inproj",
    )(x, mod3, mod3, norm_g.reshape(1, d), w_bf16)


def _alibi_slope(head):
    return 2.0 ** (-ALIBI_MAX_EXP * (head + 1) / ATT_HEADS)


def _attn_prompt_kernel(q_ref, kp_ref, kc_ref, vp_ref, vc_ref, o_ref, lse_ref, *, dil, n_taps, group):
    blk = ATT_BLOCK
    heads = LANES // ATT_HEAD_DIM
    span = blk * dil
    n_span = q_ref.shape[1] // span
    n = pl.program_id(1)
    pair = pl.program_id(2)
    i = lax.broadcasted_iota(jnp.int32, (blk, 2 * blk), 0)
    j = lax.broadcasted_iota(jnp.int32, (blk, 2 * blk), 1)
    delta = i + blk - j
    banded = (delta >= 0) & (delta <= n_taps)
    dist = (delta * dil).astype(F32)

    def rows_of(sp, r):
        return pl.ds(sp * span + r, blk, stride=dil) if dil > 1 else pl.ds(sp * span, blk)

    def block(sp, r):
        rows = rows_of(sp, r)
        q = q_ref[0, rows, :]
        if sp == 0:
            prev = rows_of(n_span - 1, r)
            k_prev, v_prev = kp_ref[0, prev, :], vp_ref[0, prev, :]
            valid = banded & ((j >= blk) | (n > 0))
        else:
            prev = rows_of(sp - 1, r)
            k_prev, v_prev = kc_ref[0, prev, :], vc_ref[0, prev, :]
            valid = banded
        kk = jnp.concatenate([k_prev, kc_ref[0, rows, :]], axis=0)
        vv = jnp.concatenate([v_prev, vc_ref[0, rows, :]], axis=0)
        outs, lses = [], []
        for h in range(heads):
            cs = slice(h * ATT_HEAD_DIM, (h + 1) * ATT_HEAD_DIM)
            slope = _alibi_slope(group * ATT_HEADS_PER_GROUP + h)
            for other in range(1, ATT_HEADS_PER_GROUP // heads):
                slope = jnp.where(pair == other, _alibi_slope(group * ATT_HEADS_PER_GROUP + other * heads + h), slope)
            s = _bdot_nt(q[:, cs], kk[:, cs]) * (ATT_HEAD_DIM ** -0.5) - slope * dist
            s = jnp.where(valid, s, NEG_INF)
            m = jnp.max(s, axis=-1, keepdims=True)
            p = jnp.exp(s - m)
            l = jnp.sum(p, axis=-1, keepdims=True)
            outs.append(_bdot(p, vv[:, cs]) / l)
            lses.append(jnp.broadcast_to(m + jnp.log(l), (blk, ATT_HEAD_DIM)))
        o_ref[0, rows, :] = jnp.concatenate(outs, axis=1)
        lse_ref[0, rows, :] = jnp.concatenate(lses, axis=1)

    per_trip = min(dil, ATT_BLOCKS_PER_TRIP)
    for sp in range(n_span):
        def trip(it, carry, sp=sp):
            for k in range(per_trip):
                block(sp, it * per_trip + k)
            return carry
        if dil == per_trip:
            trip(0, 0)
        else:
            lax.fori_loop(0, dil // per_trip, trip, 0)


def _attn_prompt(proj, group):
    window, dil = ATT_GROUPS[group]
    b, s, _ = proj.shape
    gw = ATT_GROUP_WIDTH
    span = dil * ATT_BLOCK
    tokens = max(span, min(s, ATT_TOKENS_PER_STEP))
    assert s % tokens == 0 and tokens % span == 0

    pairs = gw // LANES

    def spec(col, prev):
        if prev:
            return pl.BlockSpec((1, tokens, LANES), lambda bb, n, hp: (bb, jnp.maximum(n - 1, 0), (col + group) * pairs + hp))
        return pl.BlockSpec((1, tokens, LANES), lambda bb, n, hp: (bb, n, (col + group) * pairs + hp))

    out_spec = pl.BlockSpec((1, tokens, LANES), lambda bb, n, hp: (bb, n, hp))
    return pl.pallas_call(
        functools.partial(_attn_prompt_kernel, dil=dil, n_taps=window // dil, group=group),
        grid=(b, s // tokens, pairs),
        in_specs=[spec(COL_Q, False), spec(COL_K, True), spec(COL_K, False), spec(COL_V, True), spec(COL_V, False)],
        out_specs=[out_spec, out_spec],
        out_shape=[jax.ShapeDtypeStruct((b, s, gw), F32)] * 2,
        compiler_params=_params("arbitrary", "arbitrary", "arbitrary"),
        name=f"attn_prompt_g{group}",
    )(proj, proj, proj, proj, proj)


def _attn_sample_seq(sq, q_ref, kn_ref, vn_ref, ck_ref, cv_ref, o_ref, lse_ref, nk_ref, nv_ref, *, window, dil, group):
    t_new = q_ref.shape[1]
    w = ck_ref.shape[2]
    gw = ATT_GROUP_WIDTH
    nh = ATT_HEADS_PER_GROUP
    rows = nh * t_new
    q, kn, vn = q_ref[sq], kn_ref[sq], vn_ref[sq]
    ck, cv = ck_ref[sq], cv_ref[sq]

    r = lax.broadcasted_iota(jnp.int32, (rows, gw), 0)
    c = lax.broadcasted_iota(jnp.int32, (rows, gw), 1)
    head_mask = _div(r, t_new) == _div(c, ATT_HEAD_DIM)
    qbd = jnp.where(head_mask, jnp.concatenate([q] * nh, axis=0), 0.0)
    s_c = _bdot(qbd, ck) * (ATT_HEAD_DIM ** -0.5)
    s_n = _bdot_nt(qbd, kn) * (ATT_HEAD_DIM ** -0.5)

    hrow = _div(lax.broadcasted_iota(jnp.int32, (rows, 1), 0), t_new)
    slope = jnp.zeros((rows, 1), F32)
    for h in range(nh):
        slope = jnp.where(hrow == h, _alibi_slope(group * nh + h), slope)

    def masked(s, first_pos):
        tq = _mod(lax.broadcasted_iota(jnp.int32, s.shape, 0), t_new)
        dist = w + tq - (first_pos + lax.broadcasted_iota(jnp.int32, s.shape, 1))
        valid = (dist >= 0) & (_mod(dist, dil) == 0) & (dist <= window)
        return jnp.where(valid, s - slope * dist.astype(F32), NEG_INF)

    s_c = masked(s_c, 0)
    s_n = masked(s_n, w)
    m = jnp.maximum(jnp.max(s_c, axis=-1, keepdims=True), jnp.max(s_n, axis=-1, keepdims=True))
    p_c = jnp.exp(s_c - m)
    p_n = jnp.exp(s_n - m)
    l = jnp.sum(p_c, axis=-1, keepdims=True) + jnp.sum(p_n, axis=-1, keepdims=True)
    o_full = jnp.where(head_mask, (_bdot_nt(p_c, cv) + _bdot(p_n, vn)) / l, 0.0)
    lse_full = jnp.where(head_mask, m + jnp.log(l), 0.0)
    o, lse = o_full[0:t_new], lse_full[0:t_new]
    for h in range(1, nh):
        o = o + o_full[h * t_new:(h + 1) * t_new]
        lse = lse + lse_full[h * t_new:(h + 1) * t_new]
    o_ref[sq] = o
    lse_ref[sq] = lse

    lane = lax.broadcasted_iota(jnp.int32, (gw, LANES), 1)
    for new, old, out_ref in ((kn, ck, nk_ref), (vn, cv, nv_ref)):
        shifted = pltpu.roll(old, w - t_new, 1)
        new_t = pltpu.roll(_pad_rows(new, LANES).T, LANES - t_new, 1)
        if w > LANES:
            out_ref[sq, :, 0:w - LANES] = shifted[:, 0:w - LANES]
        out_ref[sq, :, w - LANES:w] = jnp.where(lane >= LANES - t_new, new_t, shifted[:, w - LANES:w])


def _attn_sample_kernel(*refs, **static):
    for sq in range(refs[0].shape[0]):
        _attn_sample_seq(sq, *refs, **static)


def _attn_sample(proj, cache_k, cache_v, group):
    window, dil = ATT_GROUPS[group]
    bd, t_new, _ = proj.shape
    w = cache_k.shape[2]
    gw = ATT_GROUP_WIDTH
    assert w >= window and w % LANES == 0 and t_new % SUBLANES == 0 and t_new < LANES
    bs = ATT_SEQS_PER_STEP[group]
    while bd % bs:
        bs //= 2

    def pspec(col):
        return pl.BlockSpec((bs, t_new, gw), lambda bb: (bb, 0, col + group))

    cspec = pl.BlockSpec((bs, gw, w), lambda bb: (bb, 0, 0))
    ospec = pl.BlockSpec((bs, t_new, gw), lambda bb: (bb, 0, 0))
    return pl.pallas_call(
        functools.partial(_attn_sample_kernel, window=window, dil=dil, group=group),
        grid=(bd // bs,),
        in_specs=[pspec(COL_Q), pspec(COL_K), pspec(COL_V), cspec, cspec],
        out_specs=[ospec, ospec, cspec, cspec],
        out_shape=[jax.ShapeDtypeStruct((bd, t_new, gw), F32)] * 2 + [jax.ShapeDtypeStruct((bd, gw, w), F32)] * 2,
        compiler_params=_params("arbitrary"),
        name=f"attn_sample_g{group}",
    )(proj, proj, proj, cache_k, cache_v)


def _hgrn_levels(chunk):
    levels, m = [], HG_SUB
    while m < chunk:
        levels.append(m)
        m *= 2
    return tuple(levels)


def _hgrn_kernel(zf_ref, zi_ref, zq_ref, zg_ref, lbraw_ref, gn_ref, s0_ref, o_ref, sfin_ref, st_ref, *, layer):
    nseq, chunk = zf_ref.shape[0], zf_ref.shape[1]
    chains = [(sq, hh) for sq in range(nseq) for hh in range(HG_HEADS_PER_STEP)]
    levels = _hgrn_levels(chunk)
    cidx = pl.program_id(2)
    dk, dv = HG_DK, HG_DV

    @pl.when(cidx == 0)
    def _():
        for sq, hh in chains:
            st_ref[sq, hh] = s0_ref[sq, hh].T

    raw = lbraw_ref[...]
    e = jnp.exp(raw - jnp.max(raw, axis=0, keepdims=True))
    lb_all = jnp.sum(e[0:layer + 1], axis=0, keepdims=True) / jnp.sum(e, axis=0, keepdims=True)

    rr = lax.broadcasted_iota(jnp.int32, (chunk, chunk), 0)
    cc = lax.broadcasted_iota(jnp.int32, (chunk, chunk), 1)
    sels = [cc <= rr] + [cc <= _div(rr, 2 * m) * (2 * m) + m - 1 for m in levels]
    sel = jnp.concatenate([jnp.where(sm, 1.0, 0.0) for sm in sels], axis=0).astype(BF16)

    trow = lax.broadcasted_iota(jnp.int32, (chunk, dk), 0)
    for sq, hh in chains:
        cs = slice(hh * dk, (hh + 1) * dk)
        lb = lb_all[:, cs]
        f = lb + (1.0 - lb) * jax.nn.sigmoid(zf_ref[sq][:, cs])
        lf = jnp.log(f)
        kh = 1.0 - f
        q = zq_ref[sq][:, cs]
        v = zi_ref[sq][:, cs]

        hi = lf.astype(BF16)
        r1 = lf - hi.astype(F32)
        mid = r1.astype(BF16)
        lo = (r1 - mid.astype(F32)).astype(BF16)
        bb = jnp.dot(sel, jnp.concatenate([hi, mid, lo], axis=1), preferred_element_type=F32)
        bb = bb[:, 0:dk] + bb[:, dk:2 * dk] + bb[:, 2 * dk:3 * dk]
        b = bb[0:chunk]

        st = st_ref[sq, hh]
        o = _bdot_nt(q * jnp.exp(b), st)

        for off in range(HG_SUB):
            if off == 0:
                a = jnp.sum(q * kh, axis=-1, keepdims=True)
                o = o + a * v
            else:
                in_block = _mod(trow, HG_SUB) >= off
                dec = jnp.exp(jnp.where(in_block, b - pltpu.roll(b, off, 0), NEG_INF))
                a = jnp.sum(q * pltpu.roll(kh, off, 0) * dec, axis=-1, keepdims=True)
                o = o + a * pltpu.roll(v, off, 0)

        if levels:
            ti = lax.broadcasted_iota(jnp.int32, (chunk, chunk), 0)
            si = lax.broadcasted_iota(jnp.int32, (chunk, chunk), 1)
            amat = jnp.zeros((chunk, chunk), F32)
            for li, m in enumerate(levels):
                bm = bb[(li + 1) * chunk:(li + 2) * chunk]
                second = _mod(_div(trow, m), 2) == 1
                qs = q * jnp.exp(jnp.where(second, b - bm, NEG_INF))
                ks = kh * jnp.exp(jnp.where(second, NEG_INF, bm - b))
                amat = amat + jnp.where(_div(ti, 2 * m) == _div(si, 2 * m), _bdot_nt(qs, ks), 0.0)
            o = o + _bdot(amat, v)

        b_end = b[chunk - 1:chunk]
        kd = kh * jnp.exp(b_end - b)
        vt = _pad_rows(v, dv).T
        st_ref[sq, hh] = st * jnp.exp(b_end) + _bdot(vt, _pad_rows(kd, dv))

        zg = zg_ref[sq][:, cs]
        o_ref[sq, :, cs] = _rms(o) * gn_ref[...] * (zg * jax.nn.sigmoid(zg))

    @pl.when(cidx == pl.num_programs(2) - 1)
    def _():
        for sq, hh in chains:
            sfin_ref[sq, hh] = st_ref[sq, hh].T


def _hgrn(proj, hg_lower_bound, hg_norm_g, s0, layer):
    b, l, _ = proj.shape
    assert HG_DK == HG_DV and HG_DV == LANES
    chunk = min(HG_CHUNK, l)
    assert l % chunk == 0 and chunk % HG_SUB == 0
    hps = HG_HEADS_PER_STEP
    wblk = hps * HG_DK
    bs = HG_SEQS_PER_STEP
    while b % bs:
        bs //= 2

    def pspec(col):
        return pl.BlockSpec((bs, chunk, wblk), lambda bb, hp, c: (bb, c, col + hp))

    sspec = pl.BlockSpec((bs, hps, HG_DK, HG_DV), lambda bb, hp, c: (bb, hp, 0, 0))
    return pl.pallas_call(
        functools.partial(_hgrn_kernel, layer=layer),
        grid=(b // bs, HG_HEADS // hps, l // chunk),
        in_specs=[pspec(COL_ZF), pspec(COL_ZI), pspec(COL_ZQ), pspec(COL_ZG),
                  pl.BlockSpec((hg_lower_bound.shape[0], wblk), lambda bb, hp, c: (0, hp)),
                  pl.BlockSpec((1, HG_DV), lambda bb, hp, c: (0, 0)),
                  sspec],
        out_specs=[pl.BlockSpec((bs, chunk, wblk), lambda bb, hp, c: (bb, c, hp)), sspec],
        out_shape=[jax.ShapeDtypeStruct((b, l, HG_HEADS * HG_DV), F32),
                   jax.ShapeDtypeStruct(s0.shape, F32)],
        scratch_shapes=[pltpu.VMEM((bs, hps, HG_DV, HG_DK), F32)],
        compiler_params=_params("arbitrary", "arbitrary", "arbitrary"),
        name="hgrn2",
    )(proj, proj, proj, proj, hg_lower_bound, hg_norm_g.reshape(1, HG_DV), s0)


def _merge_kernel(x_ref, o0_ref, l0_ref, o1_ref, l1_ref, o2_ref, l2_ref, oh_ref, za_ref, zh_ref,
                  g1_ref, sh2_ref, sc2_ref, n2_ref, wa_ref, wh_ref, wo_ref, x1_ref, h2_ref):
    nb, lb, d = x_ref.shape
    tm = nb * lb

    def flat(ref):
        return ref[...].reshape(tm, ref.shape[-1])

    l0, l1, l2 = flat(l0_ref), flat(l1_ref), flat(l2_ref)
    m = jnp.maximum(jnp.maximum(l0, l1), l2)
    e0, e1, e2 = jnp.exp(l0 - m), jnp.exp(l1 - m), jnp.exp(l2 - m)
    att = (e0 * flat(o0_ref) + e1 * flat(o1_ref) + e2 * flat(o2_ref)) / (e0 + e1 + e2)
    merged = (jax.nn.sigmoid(flat(za_ref)) * _bdot(att, wa_ref[...])
              + jax.nn.sigmoid(flat(zh_ref)) * _bdot(flat(oh_ref), wh_ref[...]))
    mix = _bdot(merged, wo_ref[...])
    x1 = x_ref[...] + g1_ref[...] * mix.reshape(nb, lb, d)
    x1_ref[...] = x1
    h2 = _rms(x1) * n2_ref[...] * (1.0 + sc2_ref[...]) + sh2_ref[...]
    h2_ref[...] = h2.reshape(tm, d).astype(BF16)


def _merge(x, att_o, att_lse, o_h, proj, mod3, norm2_g, wa, wh, wo):
    n, l, d = x.shape
    nb, lb = _token_blocks(n, l)
    gw = ATT_GROUP_WIDTH
    tpb = l // lb

    def tok(width, col=0):
        return pl.BlockSpec((nb, lb, width), lambda i, t: (i, t, col))

    def modspec(col):
        return pl.BlockSpec((nb, 1, d), lambda i, t: (i, 0, col))

    def full(a):
        return pl.BlockSpec(a.shape, lambda i, t: (0, 0))

    att_specs, att_args = [], []
    for o, lse in zip(att_o, att_lse):
        att_specs += [tok(gw), tok(gw)]
        att_args += [o, lse]
    return pl.pallas_call(
        _merge_kernel,
        grid=(n // nb, tpb),
        in_specs=[tok(d)] + att_specs + [tok(o_h.shape[-1]), tok(d, COL_ZA), tok(d, COL_ZH),
                                         modspec(2), modspec(3), modspec(4),
                                         pl.BlockSpec((1, d), lambda i, t: (0, 0)), full(wa), full(wh), full(wo)],
        out_specs=[tok(d), pl.BlockSpec((nb * lb, d), lambda i, t: (i * tpb + t, 0))],
        out_shape=[jax.ShapeDtypeStruct((n, l, d), F32), jax.ShapeDtypeStruct((n * l, d), BF16)],
        compiler_params=_params("arbitrary", "arbitrary"),
        name="merge",
    )(x, *att_args, o_h, proj, proj, mod3, mod3, mod3, norm2_g.reshape(1, d), wa, wh, wo)


def _top16(arrays):
    krow = lax.broadcasted_iota(jnp.int32, (PEER_TOPK, LANES), 0)

    def body(i, carry):
        out = []
        for s, vals, idxs in carry:
            rows = s.shape[0]
            row = lax.broadcasted_iota(jnp.int32, s.shape, 0)
            m = jnp.max(s, axis=0, keepdims=True)
            idx = jnp.min(jnp.where(s == m, row, rows), axis=0, keepdims=True)
            s = jnp.where(row == idx, NEG_INF, s)
            vals = jnp.where(krow == i, m, vals)
            idxs = jnp.where(krow == i, idx, idxs)
            out.append((s, vals, idxs))
        return tuple(out)

    init = tuple((s, jnp.zeros((PEER_TOPK, LANES), F32), jnp.zeros((PEER_TOPK, LANES), jnp.int32)) for s in arrays)
    return [(vals, idxs) for _, vals, idxs in lax.fori_loop(0, PEER_TOPK, body, init)]


def _top16_distinct(arrays):
    krow = lax.broadcasted_iota(jnp.int32, (PEER_TOPK, LANES), 0)

    def body(i, carry):
        out = []
        for s, vals in carry:
            m = jnp.max(s, axis=0, keepdims=True)
            out.append((jnp.where(s == m, NEG_INF, s), jnp.where(krow == i, m, vals)))
        return tuple(out)

    init = tuple((s, jnp.zeros((PEER_TOPK, LANES), F32)) for s in arrays)
    return [vals for _, vals in lax.fori_loop(0, PEER_TOPK, body, init)]


def _count_ge(s, threshold):
    return jnp.sum(jnp.where(s >= threshold, 1.0, 0.0), axis=0, keepdims=True)


def _peer_select_kernel(h2_ref, wq_ref, sk_ref, r2_ref, t1_ref, a1_ref, a2_ref, s_ref):
    tm = h2_ref.shape[0]
    kk = PEER_TOPK
    nk = PEER_NKEYS
    qry = jnp.dot(h2_ref[...], wq_ref[...], preferred_element_type=F32).astype(BF16)
    for h in range(PEER_HEADS):
        for p in range(2):
            c0 = (2 * h + p) * PEER_DHALF
            st = _bdot_nt(sk_ref[h, p], qry[:, c0:c0 + PEER_DHALF])
            for c in range(tm // LANES):
                s_ref[2 * h + p, c] = st[:, c * LANES:(c + 1) * LANES]

    row = lax.broadcasted_iota(jnp.int32, (nk, LANES), 0)
    krow = lax.broadcasted_iota(jnp.int32, (kk, LANES), 0)

    def candidates(v1, v2):
        return jnp.concatenate([v1[0:1] + v2] + [v1[a:a + 1] + v2[0:SUBLANES] for a in range(1, SUBLANES)]
                               + [v1[SUBLANES:kk] + v2[0:1]], axis=0)

    def finish(h, c, v1, i1, v2, i2, sf, ci):
        first = jnp.where(ci < kk, 0, jnp.where(ci < kk + 7 * SUBLANES, 1 + _div(ci - kk, SUBLANES),
                                                ci - (kk + 7 * SUBLANES) + SUBLANES))
        z = jnp.sum(jnp.exp(sf - sf[0:1]), axis=0, keepdims=True)
        count = jnp.zeros((kk, LANES), F32)
        for k in range(kk):
            count = count + jnp.where(krow == first[k:k + 1], 1.0, 0.0)
        t1 = jnp.zeros((nk, LANES), F32)
        r2 = jnp.full((nk, LANES), float(kk), F32)
        for a in range(kk):
            t1 = jnp.where(row == i1[a:a + 1], count[a:a + 1], t1)
            r2 = jnp.where(row == i2[a:a + 1], float(a), r2)
        r2_ref[h, c] = r2
        t1_ref[h, c] = t1
        a1_ref[h, c] = jnp.exp(s_ref[2 * h, c] - v1[0:1]) / z
        a2_ref[h, c] = jnp.exp(s_ref[2 * h + 1, c] - v2[0:1])

    def exact_pair(c, h0):
        halves = []
        for h in (h0, h0 + 1):
            (v1, i1), (v2, i2) = _top16([s_ref[2 * h, c], s_ref[2 * h + 1, c]])
            halves.append((v1, i1, v2, i2))
        picks = _top16([candidates(v1, v2) for v1, _, v2, _ in halves])
        for h, half, (sf, ci) in zip((h0, h0 + 1), halves, picks):
            finish(h, c, *half, sf, ci)

    def distinct_pair(c, h0):
        heads = (h0, h0 + 1)
        scores = [(s_ref[2 * h, c], s_ref[2 * h + 1, c]) for h in heads]
        tops = [_top16_distinct(list(pair)) for pair in scores]
        cands = [candidates(v1, v2) for v1, v2 in tops]
        sums = _top16_distinct(cands)
        doubtful = jnp.zeros((1, LANES), F32)
        for h, (s1, s2), (v1, v2), cand, sf in zip(heads, scores, tops, cands, sums):
            picked = jnp.where(cand >= sf[kk - 1:kk], 1.0, 0.0)
            sure = ((_count_ge(s1, v1[kk - 1:kk]) == kk) & (_count_ge(s2, v2[kk - 1:kk]) == kk)
                    & (jnp.sum(picked, axis=0, keepdims=True) == kk))
            doubtful = doubtful + jnp.where(sure, 0.0, 1.0)
            count = jnp.concatenate(
                [jnp.sum(picked[0:kk], axis=0, keepdims=True)]
                + [jnp.sum(picked[kk + (a - 1) * SUBLANES:kk + a * SUBLANES], axis=0, keepdims=True)
                   for a in range(1, SUBLANES)]
                + [picked[kk + 7 * SUBLANES:]], axis=0)
            z = jnp.sum(jnp.exp(sf - sf[0:1]), axis=0, keepdims=True)
            t1 = jnp.zeros((nk, LANES), F32)
            r2 = jnp.full((nk, LANES), float(kk), F32)
            for a in range(kk):
                t1 = jnp.where(s1 == v1[a:a + 1], count[a:a + 1], t1)
                r2 = jnp.where(s2 == v2[a:a + 1], float(a), r2)
            r2_ref[h, c] = r2
            t1_ref[h, c] = t1
            a1_ref[h, c] = jnp.exp(s1 - v1[0:1]) / z
            a2_ref[h, c] = jnp.exp(s2 - v2[0:1])
        return jnp.sum(doubtful)

    def chunk_body(c, carry):
        for h0 in range(0, PEER_HEADS, 2):
            doubtful = distinct_pair(c, h0)

            @pl.when(doubtful > 0.0)
            def _():
                exact_pair(c, h0)
        return carry

    lax.fori_loop(0, tm // LANES, chunk_body, 0)


def _peer_select(h2, wq, subkeys):
    t, d = h2.shape
    tm = min(t, 512)
    assert t % tm == 0 and tm % LANES == 0
    nch = tm // LANES
    ospec = pl.BlockSpec((PEER_HEADS, nch, PEER_NKEYS, LANES), lambda i: (0, i, 0, 0))

    def shape(dtype):
        return jax.ShapeDtypeStruct((PEER_HEADS, t // LANES, PEER_NKEYS, LANES), dtype)

    return pl.pallas_call(
        _peer_select_kernel,
        grid=(t // tm,),
        in_specs=[pl.BlockSpec((tm, d), lambda i: (i, 0)),
                  pl.BlockSpec(wq.shape, lambda i: (0, 0)),
                  pl.BlockSpec(subkeys.shape, lambda i: (0, 0, 0, 0))],
        out_specs=[ospec] * 4,
        out_shape=[shape(F32)] * 4,
        scratch_shapes=[pltpu.VMEM((2 * PEER_HEADS, nch, PEER_NKEYS, LANES), F32)],
        compiler_params=_params("arbitrary"),
        name="peer_select",
    )(h2, wq, subkeys)


def _gelu_tanh(x):
    return x * (0.5 * (1.0 + jnp.tanh(0.7978845608028654 * (x + 0.044715 * (x * x * x)))))


def _peer_apply_kernel(h2_ref, u_ref, vt_ref, r2_ref, t1_ref, a1_ref, a2_ref, x1_ref, g2_ref, fn_ref,
                       y_ref, acc_ref, act_ref, gw_ref, r2b_ref, a2b_ref):
    nb, lb, d = x1_ref.shape
    tm = nb * lb
    nch = tm // LANES
    nk = PEER_NKEYS
    ne1 = PEER_E1_PER_STEP
    groups = nk // BF16_ROWS
    j = pl.program_id(2)

    @pl.when(j == 0)
    def _():
        acc_ref[...] = jnp.zeros_like(acc_ref)
        for h in range(PEER_HEADS):
            for tc in range(tm // LANES):
                r2b_ref[h, tc] = r2_ref[h, tc].astype(BF16)
                a2b_ref[h, tc] = a2_ref[h, tc].astype(BF16)

    act_ref[...] = _gelu_tanh(lax.dot_general(u_ref[...], h2_ref[...], (((1,), (1,)), ((), ())),
                                              preferred_element_type=F32)).astype(BF16)

    def bcast(tile, e):
        return jnp.broadcast_to(tile[e:e + 1], (BF16_ROWS, LANES)).astype(BF16)

    e1_rows = pl.ds(pl.multiple_of(j * ne1, ne1), ne1)
    zero = jnp.zeros((BF16_ROWS, LANES), BF16)
    for tc in range(nch):
        ls = slice(tc * LANES, (tc + 1) * LANES)
        for e in range(ne1):
            wsum = [zero] * groups
            for h in range(PEER_HEADS):
                t1 = bcast(t1_ref[h, tc, e1_rows, :], e)
                a1 = bcast(a1_ref[h, tc, e1_rows, :], e)
                for g in range(groups):
                    gs = slice(g * BF16_ROWS, (g + 1) * BF16_ROWS)
                    wsum[g] = wsum[g] + jnp.where(r2b_ref[h, tc, gs, :] < t1, a1 * a2b_ref[h, tc, gs, :], zero)
            for g in range(groups):
                rs = slice(e * nk + g * BF16_ROWS, e * nk + (g + 1) * BF16_ROWS)
                gw_ref[rs, ls] = wsum[g] * act_ref[rs, ls]
    acc_ref[...] += jnp.dot(vt_ref[...], gw_ref[...], preferred_element_type=F32)

    @pl.when(j == pl.num_programs(2) - 1)
    def _():
        ff = acc_ref[...].T.reshape(nb, lb, d)
        y_ref[...] = _rms(x1_ref[...] + g2_ref[...] * ff) * fn_ref[...]


def _peer_apply(h2, u_bf16, vt_bf16, sel, x1, mod3, final_g):
    n, l, d = x1.shape
    nb, lb = _token_blocks(n, l)
    tm = nb * lb
    tpb = l // lb
    ne1 = PEER_E1_PER_STEP
    assert ne1 == SUBLANES and tm % LANES == 0
    te = ne1 * PEER_NKEYS
    nch = tm // LANES
    selspec = pl.BlockSpec((PEER_HEADS, nch, PEER_NKEYS, LANES), lambda i, t, j: (0, i * tpb + t, 0, 0))
    tok = pl.BlockSpec((nb, lb, d), lambda i, t, j: (i, t, 0))
    return pl.pallas_call(
        _peer_apply_kernel,
        grid=(n // nb, tpb, PEER_NKEYS // ne1),
        in_specs=[pl.BlockSpec((tm, d), lambda i, t, j: (i * tpb + t, 0)),
                  pl.BlockSpec((te, d), lambda i, t, j: (j, 0)),
                  pl.BlockSpec((d, te), lambda i, t, j: (0, j)),
                  selspec, selspec, selspec, selspec,
                  tok,
                  pl.BlockSpec((nb, 1, d), lambda i, t, j: (i, 0, 5)),
                  pl.BlockSpec((1, d), lambda i, t, j: (0, 0))],
        out_specs=tok,
        out_shape=jax.ShapeDtypeStruct((n, l, d), F32),
        scratch_shapes=[pltpu.VMEM((d, tm), F32),
                        pltpu.VMEM((te, tm), BF16), pltpu.VMEM((te, tm), BF16),
                        pltpu.VMEM((PEER_HEADS, nch, PEER_NKEYS, LANES), BF16),
                        pltpu.VMEM((PEER_HEADS, nch, PEER_NKEYS, LANES), BF16)],
        compiler_params=_params("arbitrary", "arbitrary", "arbitrary"),
        name="peer_apply",
    )(h2, u_bf16, vt_bf16, *sel, x1, mod3, final_g.reshape(1, d))


def _to_rows(buf):
    n, _, w = buf.shape
    return buf.reshape(n, ATT_HEADS_PER_GROUP, ATT_HEAD_DIM, w).transpose(0, 3, 1, 2)


def _to_cols(buf):
    n, w = buf.shape[0], buf.shape[1]
    return buf.transpose(0, 2, 3, 1).reshape(n, ATT_GROUP_WIDTH, w)


def _layer(x, mod, w, caches, s0, layer, final_g):
    n, l, d = x.shape
    mod3 = mod.reshape(n, 1, N_MOD * d)
    proj = _inproj(x, mod3, w["norm1_g"], w["w_in"])
    att_o, att_lse, bufs = [], [], []
    for g, (window, _) in enumerate(ATT_GROUPS):
        if caches is None:
            o, lse = _attn_prompt(proj, g)
            keep = min(window, l)
            k0 = (COL_K + g) * ATT_GROUP_WIDTH
            v0 = (COL_V + g) * ATT_GROUP_WIDTH
            kb = proj[:, l - keep:, k0:k0 + ATT_GROUP_WIDTH].reshape(n, keep, ATT_HEADS_PER_GROUP, ATT_HEAD_DIM)
            vb = proj[:, l - keep:, v0:v0 + ATT_GROUP_WIDTH].reshape(n, keep, ATT_HEADS_PER_GROUP, ATT_HEAD_DIM)
        else:
            o, lse, kb, vb = _attn_sample(proj, _to_cols(caches[2 * g]), _to_cols(caches[2 * g + 1]), g)
            kb, vb = _to_rows(kb), _to_rows(vb)
        att_o.append(o)
        att_lse.append(lse)
        bufs += [kb, vb]
    o_h, s_fin = _hgrn(proj, w["hg_lower_bound"], w["hg_norm_g"], s0, layer)
    x1, h2 = _merge(x, att_o, att_lse, o_h, proj, mod3, w["norm2_g"], w["w_att_branch"], w["w_hg_branch"], w["w_out"])
    sel = _peer_select(h2, w["peer_wq"], w["peer_subkeys"])
    y = _peer_apply(h2, w["peer_u"], w["peer_vt"], sel, x1, mod3, final_g)
    return y, bufs, s_fin


def kernel(x_prompt, x_sample, cache_k0, cache_v0, cache_k1, cache_v1, cache_k2, cache_v2, state_hgrn, c_prompt, c_sample, ada_w, ada_b, norm1_g, w_in, w_att_branch, w_hg_branch, w_out, hg_lower_bound, hg_norm_g, norm2_g, peer_wq, peer_subkeys, peer_u, peer_v, final_norm_g):
    depth = ada_w.shape[0]
    assert depth == 1, "the final RMSNorm is fused into the last PEER kernel of a single layer"
    n_p = x_prompt.shape[0]
    layer = 0
    w = {
        "norm1_g": norm1_g[layer],
        "w_in": jnp.concatenate([w_in[layer][:, ORIG_GATE_START:], w_in[layer][:, :ORIG_GATE_START]], axis=1).astype(BF16),
        "w_att_branch": w_att_branch[layer].astype(BF16),
        "w_hg_branch": w_hg_branch[layer].astype(BF16),
        "w_out": w_out[layer].astype(BF16),
        "hg_lower_bound": hg_lower_bound,
        "hg_norm_g": hg_norm_g[layer],
        "norm2_g": norm2_g[layer],
        "peer_wq": peer_wq[layer].astype(BF16),
        "peer_subkeys": peer_subkeys[layer].astype(BF16),
        "peer_u": peer_u[layer].astype(BF16),
        "peer_vt": peer_v[layer].astype(BF16).T,
    }
    mod = _adaln(jnp.concatenate([c_prompt, c_sample], axis=0), ada_w[layer], ada_b[layer])
    s0_p = jnp.zeros((n_p,) + state_hgrn.shape[2:], F32)
    y_p, bufs_p, st_p = _layer(x_prompt, mod[:n_p], w, None, s0_p, layer, final_norm_g)
    caches = (cache_k0[layer], cache_v0[layer], cache_k1[layer], cache_v1[layer], cache_k2[layer], cache_v2[layer])
    y_s, bufs_s, st_s = _layer(x_sample, mod[n_p:], w, caches, state_hgrn[layer], layer, final_norm_g)
    return (y_p, y_s, *[b[None] for b in bufs_p], st_p[None], *[b[None] for b in bufs_s], st_s[None])
```

```python
import functools

import jax
import jax.numpy as jnp
from jax import lax
from jax.experimental import pallas as pl
from jax.experimental.pallas import tpu as pltpu

F32 = jnp.float32
BF16 = jnp.bfloat16
NEG_INF = float("-inf")
LANES = 128
SUBLANES = 8
BF16_ROWS = 16

NORM_EPS = 1e-6
N_MOD = 6

ATT_HEAD_DIM = 64
ATT_GROUPS = ((128, 1), (512, 4), (2048, 16))
ATT_HEADS_PER_GROUP = 4
ATT_HEADS = len(ATT_GROUPS) * ATT_HEADS_PER_GROUP
ATT_GROUP_WIDTH = ATT_HEADS_PER_GROUP * ATT_HEAD_DIM
ATT_BLOCK = 128
ATT_TOKENS_PER_STEP = 512
ATT_BLOCKS_PER_TRIP = 4
ATT_SEQS_PER_STEP = (8, 4, 1)
ALIBI_MAX_EXP = 8.0

HG_HEADS = 4
HG_DK = 128
HG_DV = 128
HG_CHUNK = 128
HG_SUB = SUBLANES
HG_HEADS_PER_STEP = 2
HG_SEQS_PER_STEP = 8

PEER_HEADS = 8
PEER_NKEYS = 128
PEER_DHALF = 128
PEER_TOPK = 16
PEER_E1_PER_STEP = 8
PEER_E1_PER_TRIP = 4

COL_ZA, COL_ZH = 0, 1
COL_Q, COL_K, COL_V = 8, 11, 14
COL_ZF, COL_ZI, COL_ZQ, COL_ZG = 17, 19, 21, 23
ORIG_GATE_START = 4352

VMEM_LIMIT = 48 * 1024 * 1024


def _bdot(a, b):
    return jnp.dot(a.astype(BF16), b.astype(BF16), preferred_element_type=F32)


def _bdot_nt(a, b):
    return lax.dot_general(a.astype(BF16), b.astype(BF16), (((1,), (1,)), ((), ())),
                           preferred_element_type=F32)


def _log2(k):
    assert k > 0 and k & (k - 1) == 0, k
    return k.bit_length() - 1


def _div(x, k):
    return x >> _log2(k)


def _mod(x, k):
    return x & ((1 << _log2(k)) - 1)


def _rms(x):
    return x * lax.rsqrt(jnp.mean(x * x, axis=-1, keepdims=True) + NORM_EPS)


def _pad_rows(a, rows):
    if a.shape[0] == rows:
        return a
    return jnp.concatenate([a, jnp.zeros((rows - a.shape[0], a.shape[1]), a.dtype)], axis=0)


def _params(*sem):
    return pltpu.CompilerParams(dimension_semantics=sem, vmem_limit_bytes=VMEM_LIMIT)


def _adaln_kernel(c_ref, w_ref, b_ref, o_ref):
    c = c_ref[...]
    o_ref[...] = _bdot(c * jax.nn.sigmoid(c), w_ref[...]) + b_ref[...]


def _adaln(c, ada_w, ada_b):
    n, d = c.shape
    cols = ada_w.shape[1]
    tn = 1536
    return pl.pallas_call(
        _adaln_kernel,
        grid=(cols // tn,),
        in_specs=[pl.BlockSpec((n, d), lambda j: (0, 0)),
                  pl.BlockSpec((d, tn), lambda j: (0, j)),
                  pl.BlockSpec((1, tn), lambda j: (0, j))],
        out_specs=pl.BlockSpec((n, tn), lambda j: (0, j)),
        out_shape=jax.ShapeDtypeStruct((n, cols), F32),
        compiler_params=_params("arbitrary"),
        name="adaln",
    )(c, ada_w, ada_b.reshape(1, cols))


def _inproj_kernel(x_ref, sh_ref, sc_ref, g_ref, w_ref, o_ref, h_ref):
    nb, lb, d = x_ref.shape

    @pl.when(pl.program_id(2) == 0)
    def _():
        h = _rms(x_ref[...]) * g_ref[...] * (1.0 + sc_ref[...]) + sh_ref[...]
        h_ref[...] = h.reshape(nb * lb, d).astype(BF16)

    o = jnp.dot(h_ref[...], w_ref[...], preferred_element_type=F32)
    o_ref[...] = o.reshape(nb, lb, o.shape[-1])


def _token_blocks(n, l):
    if l >= 512:
        return 1, 512
    return min(n, 512 // l), l


def _inproj(x, mod3, norm_g, w_bf16):
    n, l, d = x.shape
    cols = w_bf16.shape[1]
    nb, lb = _token_blocks(n, l)
    tn = 1280
    return pl.pallas_call(
        _inproj_kernel,
        grid=(n // nb, l // lb, cols // tn),
        in_specs=[pl.BlockSpec((nb, lb, d), lambda i, t, j: (i, t, 0)),
                  pl.BlockSpec((nb, 1, d), lambda i, t, j: (i, 0, 0)),
                  pl.BlockSpec((nb, 1, d), lambda i, t, j: (i, 0, 1)),
                  pl.BlockSpec((1, d), lambda i, t, j: (0, 0)),
                  pl.BlockSpec((d, tn), lambda i, t, j: (0, j))],
        out_specs=pl.BlockSpec((nb, lb, tn), lambda i, t, j: (i, t, j)),
        out_shape=jax.ShapeDtypeStruct((n, l, cols), F32),
        scratch_shapes=[pltpu.VMEM((nb * lb, d), BF16)],
        compiler_params=_params("arbitrary", "arbitrary", "arbitrary"),
        name="inproj",
    )(x, mod3, mod3, norm_g.reshape(1, d), w_bf16)


def _alibi_slope(head):
    return 2.0 ** (-ALIBI_MAX_EXP * (head + 1) / ATT_HEADS)


def _attn_prompt_kernel(q_ref, kp_ref, kc_ref, vp_ref, vc_ref, o_ref, lse_ref, *, dil, n_taps, group):
    blk = ATT_BLOCK
    heads = LANES // ATT_HEAD_DIM
    span = blk * dil
    n_span = q_ref.shape[1] // span
    n = pl.program_id(1)
    pair = pl.program_id(2)
    i = lax.broadcasted_iota(jnp.int32, (blk, 2 * blk), 0)
    j = lax.broadcasted_iota(jnp.int32, (blk, 2 * blk), 1)
    delta = i + blk - j
    banded = (delta >= 0) & (delta <= n_taps)
    dist = (delta * dil).astype(F32)

    def rows_of(sp, r):
        return pl.ds(sp * span + r, blk, stride=dil) if dil > 1 else pl.ds(sp * span, blk)

    def block(sp, r):
        rows = rows_of(sp, r)
        q = q_ref[0, rows, :]
        if sp == 0:
            prev = rows_of(n_span - 1, r)
            k_prev, v_prev = kp_ref[0, prev, :], vp_ref[0, prev, :]
            valid = banded & ((j >= blk) | (n > 0))
        else:
            prev = rows_of(sp - 1, r)
            k_prev, v_prev = kc_ref[0, prev, :], vc_ref[0, prev, :]
            valid = banded
        kk = jnp.concatenate([k_prev, kc_ref[0, rows, :]], axis=0)
        vv = jnp.concatenate([v_prev, vc_ref[0, rows, :]], axis=0)
        outs, lses = [], []
        for h in range(heads):
            cs = slice(h * ATT_HEAD_DIM, (h + 1) * ATT_HEAD_DIM)
            slope = _alibi_slope(group * ATT_HEADS_PER_GROUP + h)
            for other in range(1, ATT_HEADS_PER_GROUP // heads):
                slope = jnp.where(pair == other, _alibi_slope(group * ATT_HEADS_PER_GROUP + other * heads + h), slope)
            s = _bdot_nt(q[:, cs], kk[:, cs]) * (ATT_HEAD_DIM ** -0.5) - slope * dist
            s = jnp.where(valid, s, NEG_INF)
            m = jnp.max(s, axis=-1, keepdims=True)
            p = jnp.exp(s - m)
            l = jnp.sum(p, axis=-1, keepdims=True)
            outs.append(_bdot(p, vv[:, cs]) / l)
            lses.append(jnp.broadcast_to(m + jnp.log(l), (blk, ATT_HEAD_DIM)))
        o_ref[0, rows, :] = jnp.concatenate(outs, axis=1)
        lse_ref[0, rows, :] = jnp.concatenate(lses, axis=1)

    per_trip = min(dil, ATT_BLOCKS_PER_TRIP)
    for sp in range(n_span):
        def trip(it, carry, sp=sp):
            for k in range(per_trip):
                block(sp, it * per_trip + k)
            return carry
        if dil == per_trip:
            trip(0, 0)
        else:
            lax.fori_loop(0, dil // per_trip, trip, 0)


def _attn_prompt(proj, group):
    window, dil = ATT_GROUPS[group]
    b, s, _ = proj.shape
    gw = ATT_GROUP_WIDTH
    span = dil * ATT_BLOCK
    tokens = max(span, min(s, ATT_TOKENS_PER_STEP))
    assert s % tokens == 0 and tokens % span == 0

    pairs = gw // LANES

    def spec(col, prev):
        if prev:
            return pl.BlockSpec((1, tokens, LANES), lambda bb, n, hp: (bb, jnp.maximum(n - 1, 0), (col + group) * pairs + hp))
        return pl.BlockSpec((1, tokens, LANES), lambda bb, n, hp: (bb, n, (col + group) * pairs + hp))

    out_spec = pl.BlockSpec((1, tokens, LANES), lambda bb, n, hp: (bb, n, hp))
    return pl.pallas_call(
        functools.partial(_attn_prompt_kernel, dil=dil, n_taps=window // dil, group=group),
        grid=(b, s // tokens, pairs),
        in_specs=[spec(COL_Q, False), spec(COL_K, True), spec(COL_K, False), spec(COL_V, True), spec(COL_V, False)],
        out_specs=[out_spec, out_spec],
        out_shape=[jax.ShapeDtypeStruct((b, s, gw), F32)] * 2,
        compiler_params=_params("arbitrary", "arbitrary", "arbitrary"),
        name=f"attn_prompt_g{group}",
    )(proj, proj, proj, proj, proj)


def _attn_sample_seq(sq, q_ref, kn_ref, vn_ref, ck_ref, cv_ref, o_ref, lse_ref, nk_ref, nv_ref, *, window, dil, group):
    t_new = q_ref.shape[1]
    w = ck_ref.shape[2]
    gw = ATT_GROUP_WIDTH
    nh = ATT_HEADS_PER_GROUP
    rows = nh * t_new
    q, kn, vn = q_ref[sq], kn_ref[sq], vn_ref[sq]
    ck, cv = ck_ref[sq], cv_ref[sq]

    r = lax.broadcasted_iota(jnp.int32, (rows, gw), 0)
    c = lax.broadcasted_iota(jnp.int32, (rows, gw), 1)
    head_mask = _div(r, t_new) == _div(c, ATT_HEAD_DIM)
    qbd = jnp.where(head_mask, jnp.concatenate([q] * nh, axis=0), 0.0)
    s_c = _bdot(qbd, ck) * (ATT_HEAD_DIM ** -0.5)
    s_n = _bdot_nt(qbd, kn) * (ATT_HEAD_DIM ** -0.5)

    hrow = _div(lax.broadcasted_iota(jnp.int32, (rows, 1), 0), t_new)
    slope = jnp.zeros((rows, 1), F32)
    for h in range(nh):
        slope = jnp.where(hrow == h, _alibi_slope(group * nh + h), slope)

    def masked(s, first_pos):
        tq = _mod(lax.broadcasted_iota(jnp.int32, s.shape, 0), t_new)
        dist = w + tq - (first_pos + lax.broadcasted_iota(jnp.int32, s.shape, 1))
        valid = (dist >= 0) & (_mod(dist, dil) == 0) & (dist <= window)
        return jnp.where(valid, s - slope * dist.astype(F32), NEG_INF)

    s_c = masked(s_c, 0)
    s_n = masked(s_n, w)
    m = jnp.maximum(jnp.max(s_c, axis=-1, keepdims=True), jnp.max(s_n, axis=-1, keepdims=True))
    p_c = jnp.exp(s_c - m)
    p_n = jnp.exp(s_n - m)
    l = jnp.sum(p_c, axis=-1, keepdims=True) + jnp.sum(p_n, axis=-1, keepdims=True)
    o_full = jnp.where(head_mask, (_bdot_nt(p_c, cv) + _bdot(p_n, vn)) / l, 0.0)
    lse_full = jnp.where(head_mask, m + jnp.log(l), 0.0)
    o, lse = o_full[0:t_new], lse_full[0:t_new]
    for h in range(1, nh):
        o = o + o_full[h * t_new:(h + 1) * t_new]
        lse = lse + lse_full[h * t_new:(h + 1) * t_new]
    o_ref[sq] = o
    lse_ref[sq] = lse

    lane = lax.broadcasted_iota(jnp.int32, (gw, LANES), 1)
    for new, old, out_ref in ((kn, ck, nk_ref), (vn, cv, nv_ref)):
        shifted = pltpu.roll(old, w - t_new, 1)
        new_t = pltpu.roll(_pad_rows(new, LANES).T, LANES - t_new, 1)
        if w > LANES:
            out_ref[sq, :, 0:w - LANES] = shifted[:, 0:w - LANES]
        out_ref[sq, :, w - LANES:w] = jnp.where(lane >= LANES - t_new, new_t, shifted[:, w - LANES:w])


def _attn_sample_kernel(*refs, **static):
    for sq in range(refs[0].shape[0]):
        _attn_sample_seq(sq, *refs, **static)


def _attn_sample(proj, cache_k, cache_v, group):
    window, dil = ATT_GROUPS[group]
    bd, t_new, _ = proj.shape
    w = cache_k.shape[2]
    gw = ATT_GROUP_WIDTH
    assert w >= window and w % LANES == 0 and t_new % SUBLANES == 0 and t_new < LANES
    bs = ATT_SEQS_PER_STEP[group]
    while bd % bs:
        bs //= 2

    def pspec(col):
        return pl.BlockSpec((bs, t_new, gw), lambda bb: (bb, 0, col + group))

    cspec = pl.BlockSpec((bs, gw, w), lambda bb: (bb, 0, 0))
    ospec = pl.BlockSpec((bs, t_new, gw), lambda bb: (bb, 0, 0))
    return pl.pallas_call(
        functools.partial(_attn_sample_kernel, window=window, dil=dil, group=group),
        grid=(bd // bs,),
        in_specs=[pspec(COL_Q), pspec(COL_K), pspec(COL_V), cspec, cspec],
        out_specs=[ospec, ospec, cspec, cspec],
        out_shape=[jax.ShapeDtypeStruct((bd, t_new, gw), F32)] * 2 + [jax.ShapeDtypeStruct((bd, gw, w), F32)] * 2,
        compiler_params=_params("arbitrary"),
        name=f"attn_sample_g{group}",
    )(proj, proj, proj, cache_k, cache_v)


def _hgrn_levels(chunk):
    levels, m = [], HG_SUB
    while m < chunk:
        levels.append(m)
        m *= 2
    return tuple(levels)


def _hgrn_kernel(zf_ref, zi_ref, zq_ref, zg_ref, lbraw_ref, gn_ref, s0_ref, o_ref, sfin_ref, st_ref, *, layer):
    nseq, chunk = zf_ref.shape[0], zf_ref.shape[1]
    chains = [(sq, hh) for sq in range(nseq) for hh in range(HG_HEADS_PER_STEP)]
    levels = _hgrn_levels(chunk)
    cidx = pl.program_id(2)
    dk, dv = HG_DK, HG_DV

    @pl.when(cidx == 0)
    def _():
        for sq, hh in chains:
            st_ref[sq, hh] = s0_ref[sq, hh].T

    raw = lbraw_ref[...]
    e = jnp.exp(raw - jnp.max(raw, axis=0, keepdims=True))
    lb_all = jnp.sum(e[0:layer + 1], axis=0, keepdims=True) / jnp.sum(e, axis=0, keepdims=True)

    rr = lax.broadcasted_iota(jnp.int32, (chunk, chunk), 0)
    cc = lax.broadcasted_iota(jnp.int32, (chunk, chunk), 1)
    sels = [cc <= rr] + [cc <= _div(rr, 2 * m) * (2 * m) + m - 1 for m in levels]
    sel = jnp.concatenate([jnp.where(sm, 1.0, 0.0) for sm in sels], axis=0).astype(BF16)

    trow = lax.broadcasted_iota(jnp.int32, (chunk, dk), 0)
    for sq, hh in chains:
        cs = slice(hh * dk, (hh + 1) * dk)
        lb = lb_all[:, cs]
        f = lb + (1.0 - lb) * jax.nn.sigmoid(zf_ref[sq][:, cs])
        lf = jnp.log(f)
        kh = 1.0 - f
        q = zq_ref[sq][:, cs]
        v = zi_ref[sq][:, cs]

        hi = lf.astype(BF16)
        r1 = lf - hi.astype(F32)
        mid = r1.astype(BF16)
        lo = (r1 - mid.astype(F32)).astype(BF16)
        bb = jnp.dot(sel, jnp.concatenate([hi, mid, lo], axis=1), preferred_element_type=F32)
        bb = bb[:, 0:dk] + bb[:, dk:2 * dk] + bb[:, 2 * dk:3 * dk]
        b = bb[0:chunk]

        st = st_ref[sq, hh]
        o = _bdot_nt(q * jnp.exp(b), st)

        for off in range(HG_SUB):
            if off == 0:
                a = jnp.sum(q * kh, axis=-1, keepdims=True)
                o = o + a * v
            else:
                in_block = _mod(trow, HG_SUB) >= off
                dec = jnp.exp(jnp.where(in_block, b - pltpu.roll(b, off, 0), NEG_INF))
                a = jnp.sum(q * pltpu.roll(kh, off, 0) * dec, axis=-1, keepdims=True)
                o = o + a * pltpu.roll(v, off, 0)

        if levels:
            ti = lax.broadcasted_iota(jnp.int32, (chunk, chunk), 0)
            si = lax.broadcasted_iota(jnp.int32, (chunk, chunk), 1)
            amat = jnp.zeros((chunk, chunk), F32)
            for li, m in enumerate(levels):
                bm = bb[(li + 1) * chunk:(li + 2) * chunk]
                second = _mod(_div(trow, m), 2) == 1
                qs = q * jnp.exp(jnp.where(second, b - bm, NEG_INF))
                ks = kh * jnp.exp(jnp.where(second, NEG_INF, bm - b))
                amat = amat + jnp.where(_div(ti, 2 * m) == _div(si, 2 * m), _bdot_nt(qs, ks), 0.0)
            o = o + _bdot(amat, v)

        b_end = b[chunk - 1:chunk]
        kd = kh * jnp.exp(b_end - b)
        vt = _pad_rows(v, dv).T
        st_ref[sq, hh] = st * jnp.exp(b_end) + _bdot(vt, _pad_rows(kd, dv))

        zg = zg_ref[sq][:, cs]
        o_ref[sq, :, cs] = _rms(o) * gn_ref[...] * (zg * jax.nn.sigmoid(zg))

    @pl.when(cidx == pl.num_programs(2) - 1)
    def _():
        for sq, hh in chains:
            sfin_ref[sq, hh] = st_ref[sq, hh].T


def _hgrn(proj, hg_lower_bound, hg_norm_g, s0, layer):
    b, l, _ = proj.shape
    assert HG_DK == HG_DV and HG_DV == LANES
    chunk = min(HG_CHUNK, l)
    assert l % chunk == 0 and chunk % HG_SUB == 0
    hps = HG_HEADS_PER_STEP
    wblk = hps * HG_DK
    bs = HG_SEQS_PER_STEP
    while b % bs:
        bs //= 2

    def pspec(col):
        return pl.BlockSpec((bs, chunk, wblk), lambda bb, hp, c: (bb, c, col + hp))

    sspec = pl.BlockSpec((bs, hps, HG_DK, HG_DV), lambda bb, hp, c: (bb, hp, 0, 0))
    return pl.pallas_call(
        functools.partial(_hgrn_kernel, layer=layer),
        grid=(b // bs, HG_HEADS // hps, l // chunk),
        in_specs=[pspec(COL_ZF), pspec(COL_ZI), pspec(COL_ZQ), pspec(COL_ZG),
                  pl.BlockSpec((hg_lower_bound.shape[0], wblk), lambda bb, hp, c: (0, hp)),
                  pl.BlockSpec((1, HG_DV), lambda bb, hp, c: (0, 0)),
                  sspec],
        out_specs=[pl.BlockSpec((bs, chunk, wblk), lambda bb, hp, c: (bb, c, hp)), sspec],
        out_shape=[jax.ShapeDtypeStruct((b, l, HG_HEADS * HG_DV), F32),
                   jax.ShapeDtypeStruct(s0.shape, F32)],
        scratch_shapes=[pltpu.VMEM((bs, hps, HG_DV, HG_DK), F32)],
        compiler_params=_params("arbitrary", "arbitrary", "arbitrary"),
        name="hgrn2",
    )(proj, proj, proj, proj, hg_lower_bound, hg_norm_g.reshape(1, HG_DV), s0)


def _merge_kernel(x_ref, o0_ref, l0_ref, o1_ref, l1_ref, o2_ref, l2_ref, oh_ref, za_ref, zh_ref,
                  g1_ref, sh2_ref, sc2_ref, n2_ref, wa_ref, wh_ref, wo_ref, x1_ref, h2_ref):
    nb, lb, d = x_ref.shape
    tm = nb * lb

    def flat(ref):
        return ref[...].reshape(tm, ref.shape[-1])

    l0, l1, l2 = flat(l0_ref), flat(l1_ref), flat(l2_ref)
    m = jnp.maximum(jnp.maximum(l0, l1), l2)
    e0, e1, e2 = jnp.exp(l0 - m), jnp.exp(l1 - m), jnp.exp(l2 - m)
    att = (e0 * flat(o0_ref) + e1 * flat(o1_ref) + e2 * flat(o2_ref)) / (e0 + e1 + e2)
    merged = (jax.nn.sigmoid(flat(za_ref)) * _bdot(att, wa_ref[...])
              + jax.nn.sigmoid(flat(zh_ref)) * _bdot(flat(oh_ref), wh_ref[...]))
    mix = _bdot(merged, wo_ref[...])
    x1 = x_ref[...] + g1_ref[...] * mix.reshape(nb, lb, d)
    x1_ref[...] = x1
    h2 = _rms(x1) * n2_ref[...] * (1.0 + sc2_ref[...]) + sh2_ref[...]
    h2_ref[...] = h2.reshape(tm, d).astype(BF16)


def _merge(x, att_o, att_lse, o_h, proj, mod3, norm2_g, wa, wh, wo):
    n, l, d = x.shape
    nb, lb = _token_blocks(n, l)
    gw = ATT_GROUP_WIDTH
    tpb = l // lb

    def tok(width, col=0):
        return pl.BlockSpec((nb, lb, width), lambda i, t: (i, t, col))

    def modspec(col):
        return pl.BlockSpec((nb, 1, d), lambda i, t: (i, 0, col))

    def full(a):
        return pl.BlockSpec(a.shape, lambda i, t: (0, 0))

    att_specs, att_args = [], []
    for o, lse in zip(att_o, att_lse):
        att_specs += [tok(gw), tok(gw)]
        att_args += [o, lse]
    return pl.pallas_call(
        _merge_kernel,
        grid=(n // nb, tpb),
        in_specs=[tok(d)] + att_specs + [tok(o_h.shape[-1]), tok(d, COL_ZA), tok(d, COL_ZH),
                                         modspec(2), modspec(3), modspec(4),
                                         pl.BlockSpec((1, d), lambda i, t: (0, 0)), full(wa), full(wh), full(wo)],
        out_specs=[tok(d), pl.BlockSpec((nb * lb, d), lambda i, t: (i * tpb + t, 0))],
        out_shape=[jax.ShapeDtypeStruct((n, l, d), F32), jax.ShapeDtypeStruct((n * l, d), BF16)],
        compiler_params=_params("arbitrary", "arbitrary"),
        name="merge",
    )(x, *att_args, o_h, proj, proj, mod3, mod3, mod3, norm2_g.reshape(1, d), wa, wh, wo)


def _top16(arrays):
    krow = lax.broadcasted_iota(jnp.int32, (PEER_TOPK, LANES), 0)

    def body(i, carry):
        out = []
        for s, vals, idxs in carry:
            rows = s.shape[0]
            row = lax.broadcasted_iota(jnp.int32, s.shape, 0)
            m = jnp.max(s, axis=0, keepdims=True)
            idx = jnp.min(jnp.where(s == m, row, rows), axis=0, keepdims=True)
            s = jnp.where(row == idx, NEG_INF, s)
            vals = jnp.where(krow == i, m, vals)
            idxs = jnp.where(krow == i, idx, idxs)
            out.append((s, vals, idxs))
        return tuple(out)

    init = tuple((s, jnp.zeros((PEER_TOPK, LANES), F32), jnp.zeros((PEER_TOPK, LANES), jnp.int32)) for s in arrays)
    return [(vals, idxs) for _, vals, idxs in lax.fori_loop(0, PEER_TOPK, body, init)]


def _top16_distinct(arrays):
    krow = lax.broadcasted_iota(jnp.int32, (PEER_TOPK, LANES), 0)

    def body(i, carry):
        out = []
        for s, vals in carry:
            m = jnp.max(s, axis=0, keepdims=True)
            out.append((jnp.where(s == m, NEG_INF, s), jnp.where(krow == i, m, vals)))
        return tuple(out)

    init = tuple((s, jnp.zeros((PEER_TOPK, LANES), F32)) for s in arrays)
    return [vals for _, vals in lax.fori_loop(0, PEER_TOPK, body, init)]


def _count_ge(s, threshold):
    return jnp.sum(jnp.where(s >= threshold, 1.0, 0.0), axis=0, keepdims=True)


def _peer_select_kernel(h2_ref, wq_ref, sk_ref, r2_ref, t1_ref, a1_ref, a2_ref, s_ref):
    tm = h2_ref.shape[0]
    kk = PEER_TOPK
    nk = PEER_NKEYS
    qry = jnp.dot(h2_ref[...], wq_ref[...], preferred_element_type=F32).astype(BF16)
    for h in range(PEER_HEADS):
        for p in range(2):
            c0 = (2 * h + p) * PEER_DHALF
            st = _bdot_nt(sk_ref[h, p], qry[:, c0:c0 + PEER_DHALF])
            for c in range(tm // LANES):
                s_ref[2 * h + p, c] = st[:, c * LANES:(c + 1) * LANES]

    row = lax.broadcasted_iota(jnp.int32, (nk, LANES), 0)
    krow = lax.broadcasted_iota(jnp.int32, (kk, LANES), 0)

    def candidates(v1, v2):
        return jnp.concatenate([v1[0:1] + v2] + [v1[a:a + 1] + v2[0:SUBLANES] for a in range(1, SUBLANES)]
                               + [v1[SUBLANES:kk] + v2[0:1]], axis=0)

    def finish(h, c, v1, i1, v2, i2, sf, ci):
        first = jnp.where(ci < kk, 0, jnp.where(ci < kk + 7 * SUBLANES, 1 + _div(ci - kk, SUBLANES),
                                                ci - (kk + 7 * SUBLANES) + SUBLANES))
        z = jnp.sum(jnp.exp(sf - sf[0:1]), axis=0, keepdims=True)
        count = jnp.zeros((kk, LANES), F32)
        for k in range(kk):
            count = count + jnp.where(krow == first[k:k + 1], 1.0, 0.0)
        t1 = jnp.zeros((nk, LANES), F32)
        r2 = jnp.full((nk, LANES), float(kk), F32)
        for a in range(kk):
            t1 = jnp.where(row == i1[a:a + 1], count[a:a + 1], t1)
            r2 = jnp.where(row == i2[a:a + 1], float(a), r2)
        r2_ref[h, c] = r2
        t1_ref[h, c] = t1
        a1_ref[h, c] = jnp.exp(s_ref[2 * h, c] - v1[0:1]) / z
        a2_ref[h, c] = jnp.exp(s_ref[2 * h + 1, c] - v2[0:1])

    def exact_pair(c, h0):
        halves = []
        for h in (h0, h0 + 1):
            (v1, i1), (v2, i2) = _top16([s_ref[2 * h, c], s_ref[2 * h + 1, c]])
            halves.append((v1, i1, v2, i2))
        picks = _top16([candidates(v1, v2) for v1, _, v2, _ in halves])
        for h, half, (sf, ci) in zip((h0, h0 + 1), halves, picks):
            finish(h, c, *half, sf, ci)

    def distinct_pair(c, h0):
        heads = (h0, h0 + 1)
        scores = [(s_ref[2 * h, c], s_ref[2 * h + 1, c]) for h in heads]
        tops = [_top16_distinct(list(pair)) for pair in scores]
        cands = [candidates(v1, v2) for v1, v2 in tops]
        sums = _top16_distinct(cands)
        doubtful = jnp.zeros((1, LANES), F32)
        for h, (s1, s2), (v1, v2), cand, sf in zip(heads, scores, tops, cands, sums):
            picked = jnp.where(cand >= sf[kk - 1:kk], 1.0, 0.0)
            sure = ((_count_ge(s1, v1[kk - 1:kk]) == kk) & (_count_ge(s2, v2[kk - 1:kk]) == kk)
                    & (jnp.sum(picked, axis=0, keepdims=True) == kk))
            doubtful = doubtful + jnp.where(sure, 0.0, 1.0)
            count = jnp.concatenate(
                [jnp.sum(picked[0:kk], axis=0, keepdims=True)]
                + [jnp.sum(picked[kk + (a - 1) * SUBLANES:kk + a * SUBLANES], axis=0, keepdims=True)
                   for a in range(1, SUBLANES)]
                + [picked[kk + 7 * SUBLANES:]], axis=0)
            z = jnp.sum(jnp.exp(sf - sf[0:1]), axis=0, keepdims=True)
            t1 = jnp.zeros((nk, LANES), F32)
            r2 = jnp.full((nk, LANES), float(kk), F32)
            for a in range(kk):
                t1 = jnp.where(s1 == v1[a:a + 1], count[a:a + 1], t1)
                r2 = jnp.where(s2 == v2[a:a + 1], float(a), r2)
            r2_ref[h, c] = r2
            t1_ref[h, c] = t1
            a1_ref[h, c] = jnp.exp(s1 - v1[0:1]) / z
            a2_ref[h, c] = jnp.exp(s2 - v2[0:1])
        return jnp.sum(doubtful)

    def chunk_body(c, carry):
        for h0 in range(0, PEER_HEADS, 2):
            doubtful = distinct_pair(c, h0)

            @pl.when(doubtful > 0.0)
            def _():
                exact_pair(c, h0)
        return carry

    lax.fori_loop(0, tm // LANES, chunk_body, 0)


def _peer_select(h2, wq, subkeys):
    t, d = h2.shape
    tm = min(t, 512)
    assert t % tm == 0 and tm % LANES == 0
    nch = tm // LANES
    ospec = pl.BlockSpec((PEER_HEADS, nch, PEER_NKEYS, LANES), lambda i: (0, i, 0, 0))

    def shape(dtype):
        return jax.ShapeDtypeStruct((PEER_HEADS, t // LANES, PEER_NKEYS, LANES), dtype)

    return pl.pallas_call(
        _peer_select_kernel,
        grid=(t // tm,),
        in_specs=[pl.BlockSpec((tm, d), lambda i: (i, 0)),
                  pl.BlockSpec(wq.shape, lambda i: (0, 0)),
                  pl.BlockSpec(subkeys.shape, lambda i: (0, 0, 0, 0))],
        out_specs=[ospec] * 4,
        out_shape=[shape(F32)] * 4,
        scratch_shapes=[pltpu.VMEM((2 * PEER_HEADS, nch, PEER_NKEYS, LANES), F32)],
        compiler_params=_params("arbitrary"),
        name="peer_select",
    )(h2, wq, subkeys)


def _gelu_tanh(x):
    return x * (0.5 * (1.0 + jnp.tanh(0.7978845608028654 * (x + 0.044715 * (x * x * x)))))


def _peer_apply_kernel(h2_ref, u_ref, vt_ref, r2_ref, t1_ref, a1_ref, a2_ref, x1_ref, g2_ref, fn_ref,
                       y_ref, acc_ref, act_ref, gw_ref, r2b_ref, a2b_ref, t1b_ref, a1b_ref):
    nb, lb, d = x1_ref.shape
    tm = nb * lb
    nch = tm // LANES
    nk = PEER_NKEYS
    ne1 = PEER_E1_PER_STEP
    groups = nk // BF16_ROWS
    j = pl.program_id(2)

    @pl.when(j == 0)
    def _():
        acc_ref[...] = jnp.zeros_like(acc_ref)
        for h in range(PEER_HEADS):
            for tc in range(tm // LANES):
                r2b_ref[h, tc] = r2_ref[h, tc].astype(BF16)
                a2b_ref[h, tc] = a2_ref[h, tc].astype(BF16)

    e1_rows = pl.ds(pl.multiple_of(j * ne1, ne1), ne1)
    for h in range(PEER_HEADS):
        for tc in range(nch):
            t1_tile = t1_ref[h, tc, e1_rows, :]
            a1_tile = a1_ref[h, tc, e1_rows, :]
            for e in range(ne1):
                t1b_ref[e, tc, h] = jnp.broadcast_to(t1_tile[e:e + 1], (BF16_ROWS, LANES)).astype(BF16)
                a1b_ref[e, tc, h] = jnp.broadcast_to(a1_tile[e:e + 1], (BF16_ROWS, LANES)).astype(BF16)

    zero = jnp.zeros((BF16_ROWS, LANES), BF16)
    sub = PEER_E1_PER_TRIP

    def trip(it, carry):
        r0 = pl.multiple_of(it * sub * nk, sub * nk)
        act_ref[pl.ds(r0, sub * nk), :] = _gelu_tanh(
            lax.dot_general(u_ref[pl.ds(r0, sub * nk), :], h2_ref[...], (((1,), (1,)), ((), ())),
                            preferred_element_type=F32)).astype(BF16)
        for e in range(sub):
            for tc in range(nch):
                ls = slice(tc * LANES, (tc + 1) * LANES)
                wsum = [zero] * groups
                for h in range(PEER_HEADS):
                    t1 = t1b_ref[it * sub + e, tc, h]
                    a1 = a1b_ref[it * sub + e, tc, h]
                    for g in range(groups):
                        gs = slice(g * BF16_ROWS, (g + 1) * BF16_ROWS)
                        wsum[g] = wsum[g] + jnp.where(r2b_ref[h, tc, gs, :] < t1, a1 * a2b_ref[h, tc, gs, :], zero)
                for g in range(groups):
                    rs = pl.ds(pl.multiple_of(r0 + e * nk + g * BF16_ROWS, BF16_ROWS), BF16_ROWS)
                    gw_ref[rs, ls] = wsum[g]
        return carry

    lax.fori_loop(0, ne1 // sub, trip, 0)
    acc_ref[...] += jnp.dot(vt_ref[...], gw_ref[...] * act_ref[...], preferred_element_type=F32)

    @pl.when(j == pl.num_programs(2) - 1)
    def _():
        ff = acc_ref[...].T.reshape(nb, lb, d)
        y_ref[...] = _rms(x1_ref[...] + g2_ref[...] * ff) * fn_ref[...]


def _peer_apply(h2, u_bf16, vt_bf16, sel, x1, mod3, final_g):
    n, l, d = x1.shape
    nb, lb = _token_blocks(n, l)
    tm = nb * lb
    tpb = l // lb
    ne1 = PEER_E1_PER_STEP
    assert ne1 == SUBLANES and tm % LANES == 0 and ne1 % PEER_E1_PER_TRIP == 0
    te = ne1 * PEER_NKEYS
    nch = tm // LANES
    selspec = pl.BlockSpec((PEER_HEADS, nch, PEER_NKEYS, LANES), lambda i, t, j: (0, i * tpb + t, 0, 0))
    tok = pl.BlockSpec((nb, lb, d), lambda i, t, j: (i, t, 0))
    return pl.pallas_call(
        _peer_apply_kernel,
        grid=(n // nb, tpb, PEER_NKEYS // ne1),
        in_specs=[pl.BlockSpec((tm, d), lambda i, t, j: (i * tpb + t, 0)),
                  pl.BlockSpec((te, d), lambda i, t, j: (j, 0)),
                  pl.BlockSpec((d, te), lambda i, t, j: (0, j)),
                  selspec, selspec, selspec, selspec,
                  tok,
                  pl.BlockSpec((nb, 1, d), lambda i, t, j: (i, 0, 5)),
                  pl.BlockSpec((1, d), lambda i, t, j: (0, 0))],
        out_specs=tok,
        out_shape=jax.ShapeDtypeStruct((n, l, d), F32),
        scratch_shapes=[pltpu.VMEM((d, tm), F32),
                        pltpu.VMEM((te, tm), BF16), pltpu.VMEM((te, tm), BF16),
                        pltpu.VMEM((PEER_HEADS, nch, PEER_NKEYS, LANES), BF16),
                        pltpu.VMEM((PEER_HEADS, nch, PEER_NKEYS, LANES), BF16),
                        pltpu.VMEM((ne1, nch, PEER_HEADS, BF16_ROWS, LANES), BF16),
                        pltpu.VMEM((ne1, nch, PEER_HEADS, BF16_ROWS, LANES), BF16)],
        compiler_params=_params("arbitrary", "arbitrary", "arbitrary"),
        name="peer_apply",
    )(h2, u_bf16, vt_bf16, *sel, x1, mod3, final_g.reshape(1, d))


def _to_rows(buf):
    n, _, w = buf.shape
    return buf.reshape(n, ATT_HEADS_PER_GROUP, ATT_HEAD_DIM, w).transpose(0, 3, 1, 2)


def _to_cols(buf):
    n, w = buf.shape[0], buf.shape[1]
    return buf.transpose(0, 2, 3, 1).reshape(n, ATT_GROUP_WIDTH, w)


def _layer(x, mod, w, caches, s0, layer, final_g):
    n, l, d = x.shape
    mod3 = mod.reshape(n, 1, N_MOD * d)
    proj = _inproj(x, mod3, w["norm1_g"], w["w_in"])
    att_o, att_lse, bufs = [], [], []
    for g, (window, _) in enumerate(ATT_GROUPS):
        if caches is None:
            o, lse = _attn_prompt(proj, g)
            keep = min(window, l)
            k0 = (COL_K + g) * ATT_GROUP_WIDTH
            v0 = (COL_V + g) * ATT_GROUP_WIDTH
            kb = proj[:, l - keep:, k0:k0 + ATT_GROUP_WIDTH].reshape(n, keep, ATT_HEADS_PER_GROUP, ATT_HEAD_DIM)
            vb = proj[:, l - keep:, v0:v0 + ATT_GROUP_WIDTH].reshape(n, keep, ATT_HEADS_PER_GROUP, ATT_HEAD_DIM)
        else:
            o, lse, kb, vb = _attn_sample(proj, _to_cols(caches[2 * g]), _to_cols(caches[2 * g + 1]), g)
            kb, vb = _to_rows(kb), _to_rows(vb)
        att_o.append(o)
        att_lse.append(lse)
        bufs += [kb, vb]
    o_h, s_fin = _hgrn(proj, w["hg_lower_bound"], w["hg_norm_g"], s0, layer)
    x1, h2 = _merge(x, att_o, att_lse, o_h, proj, mod3, w["norm2_g"], w["w_att_branch"], w["w_hg_branch"], w["w_out"])
    sel = _peer_select(h2, w["peer_wq"], w["peer_subkeys"])
    y = _peer_apply(h2, w["peer_u"], w["peer_vt"], sel, x1, mod3, final_g)
    return y, bufs, s_fin


def kernel(x_prompt, x_sample, cache_k0, cache_v0, cache_k1, cache_v1, cache_k2, cache_v2, state_hgrn, c_prompt, c_sample, ada_w, ada_b, norm1_g, w_in, w_att_branch, w_hg_branch, w_out, hg_lower_bound, hg_norm_g, norm2_g, peer_wq, peer_subkeys, peer_u, peer_v, final_norm_g):
    depth = ada_w.shape[0]
    assert depth == 1, "the final RMSNorm is fused into the last PEER kernel of a single layer"
    n_p = x_prompt.shape[0]
    layer = 0
    w = {
        "norm1_g": norm1_g[layer],
        "w_in": jnp.concatenate([w_in[layer][:, ORIG_GATE_START:], w_in[layer][:, :ORIG_GATE_START]], axis=1).astype(BF16),
        "w_att_branch": w_att_branch[layer].astype(BF16),
        "w_hg_branch": w_hg_branch[layer].astype(BF16),
        "w_out": w_out[layer].astype(BF16),
        "hg_lower_bound": hg_lower_bound,
        "hg_norm_g": hg_norm_g[layer],
        "norm2_g": norm2_g[layer],
        "peer_wq": peer_wq[layer].astype(BF16),
        "peer_subkeys": peer_subkeys[layer].astype(BF16),
        "peer_u": peer_u[layer].astype(BF16),
        "peer_vt": peer_v[layer].astype(BF16).T,
    }
    mod = _adaln(jnp.concatenate([c_prompt, c_sample], axis=0), ada_w[layer], ada_b[layer])
    s0_p = jnp.zeros((n_p,) + state_hgrn.shape[2:], F32)
    y_p, bufs_p, st_p = _layer(x_prompt, mod[:n_p], w, None, s0_p, layer, final_norm_g)
    caches = (cache_k0[layer], cache_v0[layer], cache_k1[layer], cache_v1[layer], cache_k2[layer], cache_v2[layer])
    y_s, bufs_s, st_s = _layer(x_sample, mod[n_p:], w, caches, state_hgrn[layer], layer, final_norm_g)
    return (y_p, y_s, *[b[None] for b in bufs_p], st_p[None], *[b[None] for b in bufs_s], st_s[None])
```

```python
import functools

import jax
import jax.numpy as jnp
from jax import lax
from jax.experimental import pallas as pl
from jax.experimental.pallas import tpu as pltpu

F32 = jnp.float32
BF16 = jnp.bfloat16
NEG_INF = float("-inf")
LANES = 128
SUBLANES = 8
BF16_ROWS = 16

NORM_EPS = 1e-6
N_MOD = 6
INPROJ_TOKENS = 1024

ATT_HEAD_DIM = 64
ATT_GROUPS = ((128, 1), (512, 4), (2048, 16))
ATT_HEADS_PER_GROUP = 4
ATT_HEADS = len(ATT_GROUPS) * ATT_HEADS_PER_GROUP
ATT_GROUP_WIDTH = ATT_HEADS_PER_GROUP * ATT_HEAD_DIM
ATT_BLOCK = 128
ATT_TOKENS_PER_STEP = 512
ATT_BLOCKS_PER_TRIP = 4
ATT_SEQS_PER_STEP = (8, 4, 1)
ALIBI_MAX_EXP = 8.0

HG_HEADS = 4
HG_DK = 128
HG_DV = 128
HG_CHUNK = 128
HG_SUB = SUBLANES
HG_HEADS_PER_STEP = 2
HG_SEQS_PER_STEP = 8

PEER_HEADS = 8
PEER_NKEYS = 128
PEER_DHALF = 128
PEER_TOPK = 16
PEER_E1_PER_STEP = 8

COL_ZA, COL_ZH = 0, 1
COL_Q, COL_K, COL_V = 8, 11, 14
COL_ZF, COL_ZI, COL_ZQ, COL_ZG = 17, 19, 21, 23
ORIG_GATE_START = 4352

VMEM_LIMIT = 48 * 1024 * 1024


def _bdot(a, b):
    return jnp.dot(a.astype(BF16), b.astype(BF16), preferred_element_type=F32)


def _bdot_nt(a, b):
    return lax.dot_general(a.astype(BF16), b.astype(BF16), (((1,), (1,)), ((), ())),
                           preferred_element_type=F32)


def _log2(k):
    assert k > 0 and k & (k - 1) == 0, k
    return k.bit_length() - 1


def _div(x, k):
    return x >> _log2(k)


def _mod(x, k):
    return x & ((1 << _log2(k)) - 1)


def _rms(x):
    return x * lax.rsqrt(jnp.mean(x * x, axis=-1, keepdims=True) + NORM_EPS)


def _pad_rows(a, rows):
    if a.shape[0] == rows:
        return a
    return jnp.concatenate([a, jnp.zeros((rows - a.shape[0], a.shape[1]), a.dtype)], axis=0)


def _params(*sem):
    return pltpu.CompilerParams(dimension_semantics=sem, vmem_limit_bytes=VMEM_LIMIT)


def _adaln_kernel(c_ref, w_ref, b_ref, o_ref):
    c = c_ref[...]
    o_ref[...] = _bdot(c * jax.nn.sigmoid(c), w_ref[...]) + b_ref[...]


def _adaln(c, ada_w, ada_b):
    n, d = c.shape
    cols = ada_w.shape[1]
    tn = 1536
    return pl.pallas_call(
        _adaln_kernel,
        grid=(cols // tn,),
        in_specs=[pl.BlockSpec((n, d), lambda j: (0, 0)),
                  pl.BlockSpec((d, tn), lambda j: (0, j)),
                  pl.BlockSpec((1, tn), lambda j: (0, j))],
        out_specs=pl.BlockSpec((n, tn), lambda j: (0, j)),
        out_shape=jax.ShapeDtypeStruct((n, cols), F32),
        compiler_params=_params("arbitrary"),
        name="adaln",
    )(c, ada_w, ada_b.reshape(1, cols))


def _inproj_kernel(x_ref, sh_ref, sc_ref, g_ref, w_ref, o_ref, h_ref):
    nb, lb, d = x_ref.shape

    @pl.when(pl.program_id(2) == 0)
    def _():
        h = _rms(x_ref[...]) * g_ref[...] * (1.0 + sc_ref[...]) + sh_ref[...]
        h_ref[...] = h.reshape(nb * lb, d).astype(BF16)

    o = jnp.dot(h_ref[...], w_ref[...], preferred_element_type=F32)
    o_ref[...] = o.reshape(nb, lb, o.shape[-1])


def _token_blocks(n, l, tokens=512):
    if l >= tokens:
        return 1, tokens
    return min(n, tokens // l), l


def _inproj(x, mod3, norm_g, w_bf16):
    n, l, d = x.shape
    cols = w_bf16.shape[1]
    nb, lb = _token_blocks(n, l, INPROJ_TOKENS)
    tn = 1280
    return pl.pallas_call(
        _inproj_kernel,
        grid=(n // nb, l // lb, cols // tn),
        in_specs=[pl.BlockSpec((nb, lb, d), lambda i, t, j: (i, t, 0)),
                  pl.BlockSpec((nb, 1, d), lambda i, t, j: (i, 0, 0)),
                  pl.BlockSpec((nb, 1, d), lambda i, t, j: (i, 0, 1)),
                  pl.BlockSpec((1, d), lambda i, t, j: (0, 0)),
                  pl.BlockSpec((d, tn), lambda i, t, j: (0, j))],
        out_specs=pl.BlockSpec((nb, lb, tn), lambda i, t, j: (i, t, j)),
        out_shape=jax.ShapeDtypeStruct((n, l, cols), F32),
        scratch_shapes=[pltpu.VMEM((nb * lb, d), BF16)],
        compiler_params=_params("arbitrary", "arbitrary", "arbitrary"),
        name="inproj",
    )(x, mod3, mod3, norm_g.reshape(1, d), w_bf16)


def _alibi_slope(head):
    return 2.0 ** (-ALIBI_MAX_EXP * (head + 1) / ATT_HEADS)


def _attn_prompt_kernel(q_ref, kp_ref, kc_ref, vp_ref, vc_ref, o_ref, lse_ref, *, dil, n_taps, group):
    blk = ATT_BLOCK
    heads = LANES // ATT_HEAD_DIM
    span = blk * dil
    n_span = q_ref.shape[1] // span
    n = pl.program_id(1)
    pair = pl.program_id(2)
    i = lax.broadcasted_iota(jnp.int32, (blk, 2 * blk), 0)
    j = lax.broadcasted_iota(jnp.int32, (blk, 2 * blk), 1)
    delta = i + blk - j
    banded = (delta >= 0) & (delta <= n_taps)
    dist = (delta * dil).astype(F32)

    def rows_of(sp, r):
        return pl.ds(sp * span + r, blk, stride=dil) if dil > 1 else pl.ds(sp * span, blk)

    def block(sp, r):
        rows = rows_of(sp, r)
        q = q_ref[0, rows, :]
        if sp == 0:
            prev = rows_of(n_span - 1, r)
            k_prev, v_prev = kp_ref[0, prev, :], vp_ref[0, prev, :]
            valid = banded & ((j >= blk) | (n > 0))
        else:
            prev = rows_of(sp - 1, r)
            k_prev, v_prev = kc_ref[0, prev, :], vc_ref[0, prev, :]
            valid = banded
        kk = jnp.concatenate([k_prev, kc_ref[0, rows, :]], axis=0)
        vv = jnp.concatenate([v_prev, vc_ref[0, rows, :]], axis=0)
        outs, lses = [], []
        for h in range(heads):
            cs = slice(h * ATT_HEAD_DIM, (h + 1) * ATT_HEAD_DIM)
            slope = _alibi_slope(group * ATT_HEADS_PER_GROUP + h)
            for other in range(1, ATT_HEADS_PER_GROUP // heads):
                slope = jnp.where(pair == other, _alibi_slope(group * ATT_HEADS_PER_GROUP + other * heads + h), slope)
            s = _bdot_nt(q[:, cs], kk[:, cs]) * (ATT_HEAD_DIM ** -0.5) - slope * dist
            s = jnp.where(valid, s, NEG_INF)
            m = jnp.max(s, axis=-1, keepdims=True)
            p = jnp.exp(s - m)
            l = jnp.sum(p, axis=-1, keepdims=True)
            outs.append(_bdot(p, vv[:, cs]) / l)
            lses.append(jnp.broadcast_to(m + jnp.log(l), (blk, ATT_HEAD_DIM)))
        o_ref[0, rows, :] = jnp.concatenate(outs, axis=1)
        lse_ref[0, rows, :] = jnp.concatenate(lses, axis=1)

    per_trip = min(dil, ATT_BLOCKS_PER_TRIP)
    for sp in range(n_span):
        def trip(it, carry, sp=sp):
            for k in range(per_trip):
                block(sp, it * per_trip + k)
            return carry
        if dil == per_trip:
            trip(0, 0)
        else:
            lax.fori_loop(0, dil // per_trip, trip, 0)


def _attn_prompt(proj, group):
    window, dil = ATT_GROUPS[group]
    b, s, _ = proj.shape
    gw = ATT_GROUP_WIDTH
    span = dil * ATT_BLOCK
    tokens = max(span, min(s, ATT_TOKENS_PER_STEP))
    assert s % tokens == 0 and tokens % span == 0

    pairs = gw // LANES

    def spec(col, prev):
        if prev:
            return pl.BlockSpec((1, tokens, LANES), lambda bb, n, hp: (bb, jnp.maximum(n - 1, 0), (col + group) * pairs + hp))
        return pl.BlockSpec((1, tokens, LANES), lambda bb, n, hp: (bb, n, (col + group) * pairs + hp))

    out_spec = pl.BlockSpec((1, tokens, LANES), lambda bb, n, hp: (bb, n, hp))
    return pl.pallas_call(
        functools.partial(_attn_prompt_kernel, dil=dil, n_taps=window // dil, group=group),
        grid=(b, s // tokens, pairs),
        in_specs=[spec(COL_Q, False), spec(COL_K, True), spec(COL_K, False), spec(COL_V, True), spec(COL_V, False)],
        out_specs=[out_spec, out_spec],
        out_shape=[jax.ShapeDtypeStruct((b, s, gw), F32)] * 2,
        compiler_params=_params("arbitrary", "arbitrary", "arbitrary"),
        name=f"attn_prompt_g{group}",
    )(proj, proj, proj, proj, proj)


def _attn_sample_seq(sq, q_ref, kn_ref, vn_ref, ck_ref, cv_ref, o_ref, lse_ref, nk_ref, nv_ref, *, window, dil, group):
    t_new = q_ref.shape[1]
    w = ck_ref.shape[2]
    gw = ATT_GROUP_WIDTH
    nh = ATT_HEADS_PER_GROUP
    rows = nh * t_new
    q, kn, vn = q_ref[sq], kn_ref[sq], vn_ref[sq]
    ck, cv = ck_ref[sq], cv_ref[sq]

    r = lax.broadcasted_iota(jnp.int32, (rows, gw), 0)
    c = lax.broadcasted_iota(jnp.int32, (rows, gw), 1)
    head_mask = _div(r, t_new) == _div(c, ATT_HEAD_DIM)
    qbd = jnp.where(head_mask, jnp.concatenate([q] * nh, axis=0), 0.0)
    s_c = _bdot(qbd, ck) * (ATT_HEAD_DIM ** -0.5)
    s_n = _bdot_nt(qbd, kn) * (ATT_HEAD_DIM ** -0.5)

    hrow = _div(lax.broadcasted_iota(jnp.int32, (rows, 1), 0), t_new)
    slope = jnp.zeros((rows, 1), F32)
    for h in range(nh):
        slope = jnp.where(hrow == h, _alibi_slope(group * nh + h), slope)

    def masked(s, first_pos):
        tq = _mod(lax.broadcasted_iota(jnp.int32, s.shape, 0), t_new)
        dist = w + tq - (first_pos + lax.broadcasted_iota(jnp.int32, s.shape, 1))
        valid = (dist >= 0) & (_mod(dist, dil) == 0) & (dist <= window)
        return jnp.where(valid, s - slope * dist.astype(F32), NEG_INF)

    s_c = masked(s_c, 0)
    s_n = masked(s_n, w)
    m = jnp.maximum(jnp.max(s_c, axis=-1, keepdims=True), jnp.max(s_n, axis=-1, keepdims=True))
    p_c = jnp.exp(s_c - m)
    p_n = jnp.exp(s_n - m)
    l = jnp.sum(p_c, axis=-1, keepdims=True) + jnp.sum(p_n, axis=-1, keepdims=True)
    o_full = jnp.where(head_mask, (_bdot_nt(p_c, cv) + _bdot(p_n, vn)) / l, 0.0)
    lse_full = jnp.where(head_mask, m + jnp.log(l), 0.0)
    o, lse = o_full[0:t_new], lse_full[0:t_new]
    for h in range(1, nh):
        o = o + o_full[h * t_new:(h + 1) * t_new]
        lse = lse + lse_full[h * t_new:(h + 1) * t_new]
    o_ref[sq] = o
    lse_ref[sq] = lse

    lane = lax.broadcasted_iota(jnp.int32, (gw, LANES), 1)
    for new, old, out_ref in ((kn, ck, nk_ref), (vn, cv, nv_ref)):
        shifted = pltpu.roll(old, w - t_new, 1)
        new_t = pltpu.roll(_pad_rows(new, LANES).T, LANES - t_new, 1)
        if w > LANES:
            out_ref[sq, :, 0:w - LANES] = shifted[:, 0:w - LANES]
        out_ref[sq, :, w - LANES:w] = jnp.where(lane >= LANES - t_new, new_t, shifted[:, w - LANES:w])


def _attn_sample_kernel(*refs, **static):
    for sq in range(refs[0].shape[0]):
        _attn_sample_seq(sq, *refs, **static)


def _attn_sample(proj, cache_k, cache_v, group):
    window, dil = ATT_GROUPS[group]
    bd, t_new, _ = proj.shape
    w = cache_k.shape[2]
    gw = ATT_GROUP_WIDTH
    assert w >= window and w % LANES == 0 and t_new % SUBLANES == 0 and t_new < LANES
    bs = ATT_SEQS_PER_STEP[group]
    while bd % bs:
        bs //= 2

    def pspec(col):
        return pl.BlockSpec((bs, t_new, gw), lambda bb: (bb, 0, col + group))

    cspec = pl.BlockSpec((bs, gw, w), lambda bb: (bb, 0, 0))
    ospec = pl.BlockSpec((bs, t_new, gw), lambda bb: (bb, 0, 0))
    return pl.pallas_call(
        functools.partial(_attn_sample_kernel, window=window, dil=dil, group=group),
        grid=(bd // bs,),
        in_specs=[pspec(COL_Q), pspec(COL_K), pspec(COL_V), cspec, cspec],
        out_specs=[ospec, ospec, cspec, cspec],
        out_shape=[jax.ShapeDtypeStruct((bd, t_new, gw), F32)] * 2 + [jax.ShapeDtypeStruct((bd, gw, w), F32)] * 2,
        compiler_params=_params("arbitrary"),
        name=f"attn_sample_g{group}",
    )(proj, proj, proj, cache_k, cache_v)


def _hgrn_levels(chunk):
    levels, m = [], HG_SUB
    while m < chunk:
        levels.append(m)
        m *= 2
    return tuple(levels)


def _hgrn_kernel(zf_ref, zi_ref, zq_ref, zg_ref, lbraw_ref, gn_ref, s0_ref, o_ref, sfin_ref, st_ref, *, layer):
    nseq, chunk = zf_ref.shape[0], zf_ref.shape[1]
    chains = [(sq, hh) for sq in range(nseq) for hh in range(HG_HEADS_PER_STEP)]
    levels = _hgrn_levels(chunk)
    cidx = pl.program_id(2)
    dk, dv = HG_DK, HG_DV

    @pl.when(cidx == 0)
    def _():
        for sq, hh in chains:
            st_ref[sq, hh] = s0_ref[sq, hh].T

    raw = lbraw_ref[...]
    e = jnp.exp(raw - jnp.max(raw, axis=0, keepdims=True))
    lb_all = jnp.sum(e[0:layer + 1], axis=0, keepdims=True) / jnp.sum(e, axis=0, keepdims=True)

    rr = lax.broadcasted_iota(jnp.int32, (chunk, chunk), 0)
    cc = lax.broadcasted_iota(jnp.int32, (chunk, chunk), 1)
    sels = [cc <= rr] + [cc <= _div(rr, 2 * m) * (2 * m) + m - 1 for m in levels]
    sel = jnp.concatenate([jnp.where(sm, 1.0, 0.0) for sm in sels], axis=0).astype(BF16)

    trow = lax.broadcasted_iota(jnp.int32, (chunk, dk), 0)
    for sq, hh in chains:
        cs = slice(hh * dk, (hh + 1) * dk)
        lb = lb_all[:, cs]
        f = lb + (1.0 - lb) * jax.nn.sigmoid(zf_ref[sq][:, cs])
        lf = jnp.log(f)
        kh = 1.0 - f
        q = zq_ref[sq][:, cs]
        v = zi_ref[sq][:, cs]

        hi = lf.astype(BF16)
        r1 = lf - hi.astype(F32)
        mid = r1.astype(BF16)
        lo = (r1 - mid.astype(F32)).astype(BF16)
        bb = jnp.dot(sel, jnp.concatenate([hi, mid, lo], axis=1), preferred_element_type=F32)
        bb = bb[:, 0:dk] + bb[:, dk:2 * dk] + bb[:, 2 * dk:3 * dk]
        b = bb[0:chunk]

        st = st_ref[sq, hh]
        o = _bdot_nt(q * jnp.exp(b), st)

        for off in range(HG_SUB):
            if off == 0:
                a = jnp.sum(q * kh, axis=-1, keepdims=True)
                o = o + a * v
            else:
                in_block = _mod(trow, HG_SUB) >= off
                dec = jnp.exp(jnp.where(in_block, b - pltpu.roll(b, off, 0), NEG_INF))
                a = jnp.sum(q * pltpu.roll(kh, off, 0) * dec, axis=-1, keepdims=True)
                o = o + a * pltpu.roll(v, off, 0)

        if levels:
            ti = lax.broadcasted_iota(jnp.int32, (chunk, chunk), 0)
            si = lax.broadcasted_iota(jnp.int32, (chunk, chunk), 1)
            amat = jnp.zeros((chunk, chunk), F32)
            for li, m in enumerate(levels):
                bm = bb[(li + 1) * chunk:(li + 2) * chunk]
                second = _mod(_div(trow, m), 2) == 1
                qs = q * jnp.exp(jnp.where(second, b - bm, NEG_INF))
                ks = kh * jnp.exp(jnp.where(second, NEG_INF, bm - b))
                amat = amat + jnp.where(_div(ti, 2 * m) == _div(si, 2 * m), _bdot_nt(qs, ks), 0.0)
            o = o + _bdot(amat, v)

        b_end = b[chunk - 1:chunk]
        kd = kh * jnp.exp(b_end - b)
        vt = _pad_rows(v, dv).T
        st_ref[sq, hh] = st * jnp.exp(b_end) + _bdot(vt, _pad_rows(kd, dv))

        zg = zg_ref[sq][:, cs]
        o_ref[sq, :, cs] = _rms(o) * gn_ref[...] * (zg * jax.nn.sigmoid(zg))

    @pl.when(cidx == pl.num_programs(2) - 1)
    def _():
        for sq, hh in chains:
            sfin_ref[sq, hh] = st_ref[sq, hh].T


def _hgrn(proj, hg_lower_bound, hg_norm_g, s0, layer):
    b, l, _ = proj.shape
    assert HG_DK == HG_DV and HG_DV == LANES
    chunk = min(HG_CHUNK, l)
    assert l % chunk == 0 and chunk % HG_SUB == 0
    hps = HG_HEADS_PER_STEP
    wblk = hps * HG_DK
    bs = HG_SEQS_PER_STEP
    while b % bs:
        bs //= 2

    def pspec(col):
        return pl.BlockSpec((bs, chunk, wblk), lambda bb, hp, c: (bb, c, col + hp))

    sspec = pl.BlockSpec((bs, hps, HG_DK, HG_DV), lambda bb, hp, c: (bb, hp, 0, 0))
    return pl.pallas_call(
        functools.partial(_hgrn_kernel, layer=layer),
        grid=(b // bs, HG_HEADS // hps, l // chunk),
        in_specs=[pspec(COL_ZF), pspec(COL_ZI), pspec(COL_ZQ), pspec(COL_ZG),
                  pl.BlockSpec((hg_lower_bound.shape[0], wblk), lambda bb, hp, c: (0, hp)),
                  pl.BlockSpec((1, HG_DV), lambda bb, hp, c: (0, 0)),
                  sspec],
        out_specs=[pl.BlockSpec((bs, chunk, wblk), lambda bb, hp, c: (bb, c, hp)), sspec],
        out_shape=[jax.ShapeDtypeStruct((b, l, HG_HEADS * HG_DV), F32),
                   jax.ShapeDtypeStruct(s0.shape, F32)],
        scratch_shapes=[pltpu.VMEM((bs, hps, HG_DV, HG_DK), F32)],
        compiler_params=_params("arbitrary", "arbitrary", "arbitrary"),
        name="hgrn2",
    )(proj, proj, proj, proj, hg_lower_bound, hg_norm_g.reshape(1, HG_DV), s0)


def _merge_kernel(x_ref, o0_ref, l0_ref, o1_ref, l1_ref, o2_ref, l2_ref, oh_ref, za_ref, zh_ref,
                  g1_ref, sh2_ref, sc2_ref, n2_ref, wa_ref, wh_ref, wo_ref, x1_ref, h2_ref):
    nb, lb, d = x_ref.shape
    tm = nb * lb

    def flat(ref):
        return ref[...].reshape(tm, ref.shape[-1])

    l0, l1, l2 = flat(l0_ref), flat(l1_ref), flat(l2_ref)
    m = jnp.maximum(jnp.maximum(l0, l1), l2)
    e0, e1, e2 = jnp.exp(l0 - m), jnp.exp(l1 - m), jnp.exp(l2 - m)
    att = (e0 * flat(o0_ref) + e1 * flat(o1_ref) + e2 * flat(o2_ref)) / (e0 + e1 + e2)
    merged = (jax.nn.sigmoid(flat(za_ref)) * _bdot(att, wa_ref[...])
              + jax.nn.sigmoid(flat(zh_ref)) * _bdot(flat(oh_ref), wh_ref[...]))
    mix = _bdot(merged, wo_ref[...])
    x1 = x_ref[...] + g1_ref[...] * mix.reshape(nb, lb, d)
    x1_ref[...] = x1
    h2 = _rms(x1) * n2_ref[...] * (1.0 + sc2_ref[...]) + sh2_ref[...]
    h2_ref[...] = h2.reshape(tm, d).astype(BF16)


def _merge(x, att_o, att_lse, o_h, proj, mod3, norm2_g, wa, wh, wo):
    n, l, d = x.shape
    nb, lb = _token_blocks(n, l)
    gw = ATT_GROUP_WIDTH
    tpb = l // lb

    def tok(width, col=0):
        return pl.BlockSpec((nb, lb, width), lambda i, t: (i, t, col))

    def modspec(col):
        return pl.BlockSpec((nb, 1, d), lambda i, t: (i, 0, col))

    def full(a):
        return pl.BlockSpec(a.shape, lambda i, t: (0, 0))

    att_specs, att_args = [], []
    for o, lse in zip(att_o, att_lse):
        att_specs += [tok(gw), tok(gw)]
        att_args += [o, lse]
    return pl.pallas_call(
        _merge_kernel,
        grid=(n // nb, tpb),
        in_specs=[tok(d)] + att_specs + [tok(o_h.shape[-1]), tok(d, COL_ZA), tok(d, COL_ZH),
                                         modspec(2), modspec(3), modspec(4),
                                         pl.BlockSpec((1, d), lambda i, t: (0, 0)), full(wa), full(wh), full(wo)],
        out_specs=[tok(d), pl.BlockSpec((nb * lb, d), lambda i, t: (i * tpb + t, 0))],
        out_shape=[jax.ShapeDtypeStruct((n, l, d), F32), jax.ShapeDtypeStruct((n * l, d), BF16)],
        compiler_params=_params("arbitrary", "arbitrary"),
        name="merge",
    )(x, *att_args, o_h, proj, proj, mod3, mod3, mod3, norm2_g.reshape(1, d), wa, wh, wo)


def _top16(arrays):
    krow = lax.broadcasted_iota(jnp.int32, (PEER_TOPK, LANES), 0)

    def body(i, carry):
        out = []
        for s, vals, idxs in carry:
            rows = s.shape[0]
            row = lax.broadcasted_iota(jnp.int32, s.shape, 0)
            m = jnp.max(s, axis=0, keepdims=True)
            idx = jnp.min(jnp.where(s == m, row, rows), axis=0, keepdims=True)
            s = jnp.where(row == idx, NEG_INF, s)
            vals = jnp.where(krow == i, m, vals)
            idxs = jnp.where(krow == i, idx, idxs)
            out.append((s, vals, idxs))
        return tuple(out)

    init = tuple((s, jnp.zeros((PEER_TOPK, LANES), F32), jnp.zeros((PEER_TOPK, LANES), jnp.int32)) for s in arrays)
    return [(vals, idxs) for _, vals, idxs in lax.fori_loop(0, PEER_TOPK, body, init)]


def _bitonic_merge_desc(v):
    n = len(v)
    v = list(v)
    j = n // 2
    while j >= 1:
        for i in range(n):
            if i & j == 0:
                v[i], v[i + j] = jnp.maximum(v[i], v[i + j]), jnp.minimum(v[i], v[i + j])
        j //= 2
    return v


def _top16_values(s):
    kk = PEER_TOPK
    rows = s.shape[0]
    assert rows % SUBLANES == 0 and rows <= kk * SUBLANES
    v = [s[SUBLANES * i:SUBLANES * (i + 1)] for i in range(rows // SUBLANES)]
    v += [jnp.full((SUBLANES, LANES), NEG_INF, F32)] * (kk - len(v))
    size = 2
    while size <= kk:
        j = size // 2
        while j >= 1:
            for i in range(kk):
                if i & j == 0:
                    hi, lo = jnp.maximum(v[i], v[i + j]), jnp.minimum(v[i], v[i + j])
                    v[i], v[i + j] = (hi, lo) if i & size == 0 else (lo, hi)
            j //= 2
        size *= 2
    for shift in (1, 2, 4):
        other = [pltpu.roll(x, SUBLANES - shift, 0) for x in v]
        v = _bitonic_merge_desc([jnp.maximum(v[i], other[kk - 1 - i]) for i in range(kk)])
    return jnp.concatenate([x[0:1] for x in v], axis=0)


def _count_ge(s, threshold):
    return jnp.sum(jnp.where(s >= threshold, 1.0, 0.0), axis=0, keepdims=True)


def _peer_select_kernel(h2_ref, wq_ref, sk_ref, r2_ref, t1_ref, a1_ref, a2_ref, s_ref):
    tm = h2_ref.shape[0]
    kk = PEER_TOPK
    nk = PEER_NKEYS
    qry = jnp.dot(h2_ref[...], wq_ref[...], preferred_element_type=F32).astype(BF16)
    for h in range(PEER_HEADS):
        for p in range(2):
            c0 = (2 * h + p) * PEER_DHALF
            st = _bdot_nt(sk_ref[h, p], qry[:, c0:c0 + PEER_DHALF])
            for c in range(tm // LANES):
                s_ref[2 * h + p, c] = st[:, c * LANES:(c + 1) * LANES]

    row = lax.broadcasted_iota(jnp.int32, (nk, LANES), 0)
    krow = lax.broadcasted_iota(jnp.int32, (kk, LANES), 0)

    def candidates(v1, v2):
        return jnp.concatenate([v1[0:1] + v2] + [v1[a:a + 1] + v2[0:SUBLANES] for a in range(1, SUBLANES)]
                               + [v1[SUBLANES:kk] + v2[0:1]], axis=0)

    def finish(h, c, v1, i1, v2, i2, sf, ci):
        first = jnp.where(ci < kk, 0, jnp.where(ci < kk + 7 * SUBLANES, 1 + _div(ci - kk, SUBLANES),
                                                ci - (kk + 7 * SUBLANES) + SUBLANES))
        z = jnp.sum(jnp.exp(sf - sf[0:1]), axis=0, keepdims=True)
        count = jnp.zeros((kk, LANES), F32)
        for k in range(kk):
            count = count + jnp.where(krow == first[k:k + 1], 1.0, 0.0)
        t1 = jnp.zeros((nk, LANES), F32)
        r2 = jnp.full((nk, LANES), float(kk), F32)
        for a in range(kk):
            t1 = jnp.where(row == i1[a:a + 1], count[a:a + 1], t1)
            r2 = jnp.where(row == i2[a:a + 1], float(a), r2)
        r2_ref[h, c] = r2
        t1_ref[h, c] = t1
        a1_ref[h, c] = jnp.exp(s_ref[2 * h, c] - v1[0:1]) / z
        a2_ref[h, c] = jnp.exp(s_ref[2 * h + 1, c] - v2[0:1])

    def exact_pair(c, h0):
        halves = []
        for h in (h0, h0 + 1):
            (v1, i1), (v2, i2) = _top16([s_ref[2 * h, c], s_ref[2 * h + 1, c]])
            halves.append((v1, i1, v2, i2))
        picks = _top16([candidates(v1, v2) for v1, _, v2, _ in halves])
        for h, half, (sf, ci) in zip((h0, h0 + 1), halves, picks):
            finish(h, c, *half, sf, ci)

    def distinct_pair(c, h0):
        heads = (h0, h0 + 1)
        scores = [(s_ref[2 * h, c], s_ref[2 * h + 1, c]) for h in heads]
        tops = [(_top16_values(s1), _top16_values(s2)) for s1, s2 in scores]
        cands = [candidates(v1, v2) for v1, v2 in tops]
        sums = [_top16_values(cand) for cand in cands]
        doubtful = jnp.zeros((1, LANES), F32)
        for h, (s1, s2), (v1, v2), cand, sf in zip(heads, scores, tops, cands, sums):
            picked = jnp.where(cand >= sf[kk - 1:kk], 1.0, 0.0)
            repeats = sum(jnp.sum(jnp.where(v[0:kk - 1] == v[1:kk], 1.0, 0.0), axis=0, keepdims=True)
                          for v in (v1, v2, sf))
            sure = ((_count_ge(s1, v1[kk - 1:kk]) == kk) & (_count_ge(s2, v2[kk - 1:kk]) == kk)
                    & (jnp.sum(picked, axis=0, keepdims=True) == kk) & (repeats == 0.0))
            doubtful = doubtful + jnp.where(sure, 0.0, 1.0)
            count = jnp.concatenate(
                [jnp.sum(picked[0:kk], axis=0, keepdims=True)]
                + [jnp.sum(picked[kk + (a - 1) * SUBLANES:kk + a * SUBLANES], axis=0, keepdims=True)
                   for a in range(1, SUBLANES)]
                + [picked[kk + 7 * SUBLANES:]], axis=0)
            z = jnp.sum(jnp.exp(sf - sf[0:1]), axis=0, keepdims=True)
            t1 = jnp.zeros((nk, LANES), F32)
            r2 = jnp.full((nk, LANES), float(kk), F32)
            for a in range(kk):
                t1 = jnp.where(s1 == v1[a:a + 1], count[a:a + 1], t1)
                r2 = jnp.where(s2 == v2[a:a + 1], float(a), r2)
            r2_ref[h, c] = r2
            t1_ref[h, c] = t1
            a1_ref[h, c] = jnp.exp(s1 - v1[0:1]) / z
            a2_ref[h, c] = jnp.exp(s2 - v2[0:1])
        return jnp.sum(doubtful)

    def chunk_body(c, carry):
        for h0 in range(0, PEER_HEADS, 2):
            doubtful = distinct_pair(c, h0)

            @pl.when(doubtful > 0.0)
            def _():
                exact_pair(c, h0)
        return carry

    lax.fori_loop(0, tm // LANES, chunk_body, 0)


def _peer_select(h2, wq, subkeys):
    t, d = h2.shape
    tm = min(t, 512)
    assert t % tm == 0 and tm % LANES == 0
    nch = tm // LANES
    ospec = pl.BlockSpec((PEER_HEADS, nch, PEER_NKEYS, LANES), lambda i: (0, i, 0, 0))

    def shape(dtype):
        return jax.ShapeDtypeStruct((PEER_HEADS, t // LANES, PEER_NKEYS, LANES), dtype)

    return pl.pallas_call(
        _peer_select_kernel,
        grid=(t // tm,),
        in_specs=[pl.BlockSpec((tm, d), lambda i: (i, 0)),
                  pl.BlockSpec(wq.shape, lambda i: (0, 0)),
                  pl.BlockSpec(subkeys.shape, lambda i: (0, 0, 0, 0))],
        out_specs=[ospec] * 4,
        out_shape=[shape(F32)] * 4,
        scratch_shapes=[pltpu.VMEM((2 * PEER_HEADS, nch, PEER_NKEYS, LANES), F32)],
        compiler_params=_params("arbitrary"),
        name="peer_select",
    )(h2, wq, subkeys)


def _gelu_tanh(x):
    return x * (0.5 * (1.0 + jnp.tanh(0.7978845608028654 * (x + 0.044715 * (x * x * x)))))


def _peer_apply_kernel(h2_ref, u_ref, vt_ref, r2_ref, t1_ref, a1_ref, a2_ref, x1_ref, g2_ref, fn_ref,
                       y_ref, acc_ref, act_ref, gw_ref, r2b_ref, a2b_ref):
    nb, lb, d = x1_ref.shape
    tm = nb * lb
    nch = tm // LANES
    nk = PEER_NKEYS
    ne1 = PEER_E1_PER_STEP
    groups = nk // BF16_ROWS
    j = pl.program_id(2)

    @pl.when(j == 0)
    def _():
        acc_ref[...] = jnp.zeros_like(acc_ref)
        for h in range(PEER_HEADS):
            for tc in range(tm // LANES):
                r2b_ref[h, tc] = r2_ref[h, tc].astype(BF16)
                a2b_ref[h, tc] = a2_ref[h, tc].astype(BF16)

    act_ref[...] = _gelu_tanh(lax.dot_general(u_ref[...], h2_ref[...], (((1,), (1,)), ((), ())),
                                              preferred_element_type=F32)).astype(BF16)

    def bcast(tile, e):
        return jnp.broadcast_to(tile[e:e + 1], (BF16_ROWS, LANES)).astype(BF16)

    e1_rows = pl.ds(pl.multiple_of(j * ne1, ne1), ne1)
    zero = jnp.zeros((BF16_ROWS, LANES), BF16)
    for tc in range(nch):
        ls = slice(tc * LANES, (tc + 1) * LANES)
        for e in range(ne1):
            wsum = [zero] * groups
            for h in range(PEER_HEADS):
                t1 = bcast(t1_ref[h, tc, e1_rows, :], e)
                a1 = bcast(a1_ref[h, tc, e1_rows, :], e)
                for g in range(groups):
                    gs = slice(g * BF16_ROWS, (g + 1) * BF16_ROWS)
                    wsum[g] = wsum[g] + jnp.where(r2b_ref[h, tc, gs, :] < t1, a1 * a2b_ref[h, tc, gs, :], zero)
            for g in range(groups):
                rs = slice(e * nk + g * BF16_ROWS, e * nk + (g + 1) * BF16_ROWS)
                gw_ref[rs, ls] = wsum[g] * act_ref[rs, ls]
    acc_ref[...] += jnp.dot(vt_ref[...], gw_ref[...], preferred_element_type=F32)

    @pl.when(j == pl.num_programs(2) - 1)
    def _():
        ff = acc_ref[...].T.reshape(nb, lb, d)
        y_ref[...] = _rms(x1_ref[...] + g2_ref[...] * ff) * fn_ref[...]


def _peer_apply(h2, u_bf16, vt_bf16, sel, x1, mod3, final_g):
    n, l, d = x1.shape
    nb, lb = _token_blocks(n, l)
    tm = nb * lb
    tpb = l // lb
    ne1 = PEER_E1_PER_STEP
    assert ne1 == SUBLANES and tm % LANES == 0
    te = ne1 * PEER_NKEYS
    nch = tm // LANES
    selspec = pl.BlockSpec((PEER_HEADS, nch, PEER_NKEYS, LANES), lambda i, t, j: (0, i * tpb + t, 0, 0))
    tok = pl.BlockSpec((nb, lb, d), lambda i, t, j: (i, t, 0))
    return pl.pallas_call(
        _peer_apply_kernel,
        grid=(n // nb, tpb, PEER_NKEYS // ne1),
        in_specs=[pl.BlockSpec((tm, d), lambda i, t, j: (i * tpb + t, 0)),
                  pl.BlockSpec((te, d), lambda i, t, j: (j, 0)),
                  pl.BlockSpec((d, te), lambda i, t, j: (0, j)),
                  selspec, selspec, selspec, selspec,
                  tok,
                  pl.BlockSpec((nb, 1, d), lambda i, t, j: (i, 0, 5)),
                  pl.BlockSpec((1, d), lambda i, t, j: (0, 0))],
        out_specs=tok,
        out_shape=jax.ShapeDtypeStruct((n, l, d), F32),
        scratch_shapes=[pltpu.VMEM((d, tm), F32),
                        pltpu.VMEM((te, tm), BF16), pltpu.VMEM((te, tm), BF16),
                        pltpu.VMEM((PEER_HEADS, nch, PEER_NKEYS, LANES), BF16),
                        pltpu.VMEM((PEER_HEADS, nch, PEER_NKEYS, LANES), BF16)],
        compiler_params=_params("arbitrary", "arbitrary", "arbitrary"),
        name="peer_apply",
    )(h2, u_bf16, vt_bf16, *sel, x1, mod3, final_g.reshape(1, d))


def _to_rows(buf):
    n, _, w = buf.shape
    return buf.reshape(n, ATT_HEADS_PER_GROUP, ATT_HEAD_DIM, w).transpose(0, 3, 1, 2)


def _to_cols(buf):
    n, w = buf.shape[0], buf.shape[1]
    return buf.transpose(0, 2, 3, 1).reshape(n, ATT_GROUP_WIDTH, w)


def _layer(x, mod, w, caches, s0, layer, final_g):
    n, l, d = x.shape
    mod3 = mod.reshape(n, 1, N_MOD * d)
    proj = _inproj(x, mod3, w["norm1_g"], w["w_in"])
    att_o, att_lse, bufs = [], [], []
    for g, (window, _) in enumerate(ATT_GROUPS):
        if caches is None:
            o, lse = _attn_prompt(proj, g)
            keep = min(window, l)
            k0 = (COL_K + g) * ATT_GROUP_WIDTH
            v0 = (COL_V + g) * ATT_GROUP_WIDTH
            kb = proj[:, l - keep:, k0:k0 + ATT_GROUP_WIDTH].reshape(n, keep, ATT_HEADS_PER_GROUP, ATT_HEAD_DIM)
            vb = proj[:, l - keep:, v0:v0 + ATT_GROUP_WIDTH].reshape(n, keep, ATT_HEADS_PER_GROUP, ATT_HEAD_DIM)
        else:
            o, lse, kb, vb = _attn_sample(proj, _to_cols(caches[2 * g]), _to_cols(caches[2 * g + 1]), g)
            kb, vb = _to_rows(kb), _to_rows(vb)
        att_o.append(o)
        att_lse.append(lse)
        bufs += [kb, vb]
    o_h, s_fin = _hgrn(proj, w["hg_lower_bound"], w["hg_norm_g"], s0, layer)
    x1, h2 = _merge(x, att_o, att_lse, o_h, proj, mod3, w["norm2_g"], w["w_att_branch"], w["w_hg_branch"], w["w_out"])
    sel = _peer_select(h2, w["peer_wq"], w["peer_subkeys"])
    y = _peer_apply(h2, w["peer_u"], w["peer_vt"], sel, x1, mod3, final_g)
    return y, bufs, s_fin


def kernel(x_prompt, x_sample, cache_k0, cache_v0, cache_k1, cache_v1, cache_k2, cache_v2, state_hgrn, c_prompt, c_sample, ada_w, ada_b, norm1_g, w_in, w_att_branch, w_hg_branch, w_out, hg_lower_bound, hg_norm_g, norm2_g, peer_wq, peer_subkeys, peer_u, peer_v, final_norm_g):
    depth = ada_w.shape[0]
    assert depth == 1, "the final RMSNorm is fused into the last PEER kernel of a single layer"
    n_p = x_prompt.shape[0]
    layer = 0
    w = {
        "norm1_g": norm1_g[layer],
        "w_in": jnp.concatenate([w_in[layer][:, ORIG_GATE_START:], w_in[layer][:, :ORIG_GATE_START]], axis=1).astype(BF16),
        "w_att_branch": w_att_branch[layer].astype(BF16),
        "w_hg_branch": w_hg_branch[layer].astype(BF16),
        "w_out": w_out[layer].astype(BF16),
        "hg_lower_bound": hg_lower_bound,
        "hg_norm_g": hg_norm_g[layer],
        "norm2_g": norm2_g[layer],
        "peer_wq": peer_wq[layer].astype(BF16),
        "peer_subkeys": peer_subkeys[layer].astype(BF16),
        "peer_u": peer_u[layer].astype(BF16),
        "peer_vt": peer_v[layer].astype(BF16).T,
    }
    mod = _adaln(jnp.concatenate([c_prompt, c_sample], axis=0), ada_w[layer], ada_b[layer])
    s0_p = jnp.zeros((n_p,) + state_hgrn.shape[2:], F32)
    y_p, bufs_p, st_p = _layer(x_prompt, mod[:n_p], w, None, s0_p, layer, final_norm_g)
    caches = (cache_k0[layer], cache_v0[layer], cache_k1[layer], cache_v1[layer], cache_k2[layer], cache_v2[layer])
    y_s, bufs_s, st_s = _layer(x_sample, mod[n_p:], w, caches, state_hgrn[layer], layer, final_norm_g)
    return (y_p, y_s, *[b[None] for b in bufs_p], st_p[None], *[b[None] for b in bufs_s], st_s[None])
```

```python
import functools

import jax
import jax.numpy as jnp
from jax import lax
from jax.experimental import pallas as pl
from jax.experimental.pallas import tpu as pltpu

F32 = jnp.float32
BF16 = jnp.bfloat16
NEG_INF = float("-inf")
LANES = 128
SUBLANES = 8
BF16_ROWS = 16

NORM_EPS = 1e-6
N_MOD = 6
INPROJ_TOKENS = 1024
INPROJ_COLS = 1280

ATT_HEAD_DIM = 64
ATT_GROUPS = ((128, 1), (512, 4), (2048, 16))
ATT_HEADS_PER_GROUP = 4
ATT_HEADS = len(ATT_GROUPS) * ATT_HEADS_PER_GROUP
ATT_GROUP_WIDTH = ATT_HEADS_PER_GROUP * ATT_HEAD_DIM
ATT_BLOCK = 128
ATT_TOKENS_PER_STEP = 512
ATT_BLOCKS_PER_TRIP = 4
ATT_SEQS_PER_STEP = (8, 4, 1)
ALIBI_MAX_EXP = 8.0

HG_HEADS = 4
HG_DK = 128
HG_DV = 128
HG_CHUNK = 128
HG_SUB = SUBLANES
HG_HEADS_PER_STEP = 2
HG_SEQS_PER_STEP = 8

PEER_HEADS = 8
PEER_NKEYS = 128
PEER_DHALF = 128
PEER_TOPK = 16
PEER_E1_PER_STEP = 16

COL_ZA, COL_ZH = 0, 1
COL_Q, COL_K, COL_V = 8, 11, 14
COL_ZF, COL_ZI, COL_ZQ, COL_ZG = 17, 19, 21, 23
ORIG_GATE_START = 4352

VMEM_LIMIT = 56 * 1024 * 1024


def _bdot(a, b):
    return jnp.dot(a.astype(BF16), b.astype(BF16), preferred_element_type=F32)


def _bdot_nt(a, b):
    return lax.dot_general(a.astype(BF16), b.astype(BF16), (((1,), (1,)), ((), ())),
                           preferred_element_type=F32)


def _log2(k):
    assert k > 0 and k & (k - 1) == 0, k
    return k.bit_length() - 1


def _div(x, k):
    return x >> _log2(k)


def _mod(x, k):
    return x & ((1 << _log2(k)) - 1)


def _rms(x):
    return x * lax.rsqrt(jnp.mean(x * x, axis=-1, keepdims=True) + NORM_EPS)


def _pad_rows(a, rows):
    if a.shape[0] == rows:
        return a
    return jnp.concatenate([a, jnp.zeros((rows - a.shape[0], a.shape[1]), a.dtype)], axis=0)


def _params(*sem):
    return pltpu.CompilerParams(dimension_semantics=sem, vmem_limit_bytes=VMEM_LIMIT)


def _adaln_kernel(c_ref, w_ref, b_ref, o_ref):
    c = c_ref[...]
    o_ref[...] = _bdot(c * jax.nn.sigmoid(c), w_ref[...]) + b_ref[...]


def _adaln(c, ada_w, ada_b):
    n, d = c.shape
    cols = ada_w.shape[1]
    tn = 1536
    return pl.pallas_call(
        _adaln_kernel,
        grid=(cols // tn,),
        in_specs=[pl.BlockSpec((n, d), lambda j: (0, 0)),
                  pl.BlockSpec((d, tn), lambda j: (0, j)),
                  pl.BlockSpec((1, tn), lambda j: (0, j))],
        out_specs=pl.BlockSpec((n, tn), lambda j: (0, j)),
        out_shape=jax.ShapeDtypeStruct((n, cols), F32),
        compiler_params=_params("arbitrary"),
        name="adaln",
    )(c, ada_w, ada_b.reshape(1, cols))


def _inproj_kernel(x_ref, sh_ref, sc_ref, g_ref, w_ref, o_ref, h_ref):
    nb, lb, d = x_ref.shape

    @pl.when(pl.program_id(2) == 0)
    def _():
        h = _rms(x_ref[...]) * g_ref[...] * (1.0 + sc_ref[...]) + sh_ref[...]
        h_ref[...] = h.reshape(nb * lb, d).astype(BF16)

    o = jnp.dot(h_ref[...], w_ref[...], preferred_element_type=F32)
    o_ref[...] = o.reshape(nb, lb, o.shape[-1])


def _token_blocks(n, l, tokens=512):
    if l >= tokens:
        return 1, tokens
    return min(n, tokens // l), l


def _inproj(x, mod3, norm_g, w_bf16):
    n, l, d = x.shape
    cols = w_bf16.shape[1]
    nb, lb = _token_blocks(n, l, INPROJ_TOKENS)
    tn = INPROJ_COLS
    return pl.pallas_call(
        _inproj_kernel,
        grid=(n // nb, l // lb, cols // tn),
        in_specs=[pl.BlockSpec((nb, lb, d), lambda i, t, j: (i, t, 0)),
                  pl.BlockSpec((nb, 1, d), lambda i, t, j: (i, 0, 0)),
                  pl.BlockSpec((nb, 1, d), lambda i, t, j: (i, 0, 1)),
                  pl.BlockSpec((1, d), lambda i, t, j: (0, 0)),
                  pl.BlockSpec((d, tn), lambda i, t, j: (0, j))],
        out_specs=pl.BlockSpec((nb, lb, tn), lambda i, t, j: (i, t, j)),
        out_shape=jax.ShapeDtypeStruct((n, l, cols), F32),
        scratch_shapes=[pltpu.VMEM((nb * lb, d), BF16)],
        compiler_params=_params("arbitrary", "arbitrary", "arbitrary"),
        name="inproj",
    )(x, mod3, mod3, norm_g.reshape(1, d), w_bf16)


def _alibi_slope(head):
    return 2.0 ** (-ALIBI_MAX_EXP * (head + 1) / ATT_HEADS)


def _attn_prompt_kernel(q_ref, kp_ref, kc_ref, vp_ref, vc_ref, o_ref, lse_ref, *, dil, n_taps, group):
    blk = ATT_BLOCK
    heads = LANES // ATT_HEAD_DIM
    span = blk * dil
    n_span = q_ref.shape[1] // span
    n = pl.program_id(1)
    pair = pl.program_id(2)
    i = lax.broadcasted_iota(jnp.int32, (blk, 2 * blk), 0)
    j = lax.broadcasted_iota(jnp.int32, (blk, 2 * blk), 1)
    delta = i + blk - j
    banded = (delta >= 0) & (delta <= n_taps)
    dist = (delta * dil).astype(F32)

    def rows_of(sp, r):
        return pl.ds(sp * span + r, blk, stride=dil) if dil > 1 else pl.ds(sp * span, blk)

    def block(sp, r):
        rows = rows_of(sp, r)
        q = q_ref[0, rows, :]
        if sp == 0:
            prev = rows_of(n_span - 1, r)
            k_prev, v_prev = kp_ref[0, prev, :], vp_ref[0, prev, :]
            valid = banded & ((j >= blk) | (n > 0))
        else:
            prev = rows_of(sp - 1, r)
            k_prev, v_prev = kc_ref[0, prev, :], vc_ref[0, prev, :]
            valid = banded
        kk = jnp.concatenate([k_prev, kc_ref[0, rows, :]], axis=0)
        vv = jnp.concatenate([v_prev, vc_ref[0, rows, :]], axis=0)
        outs, lses = [], []
        for h in range(heads):
            cs = slice(h * ATT_HEAD_DIM, (h + 1) * ATT_HEAD_DIM)
            slope = _alibi_slope(group * ATT_HEADS_PER_GROUP + h)
            for other in range(1, ATT_HEADS_PER_GROUP // heads):
                slope = jnp.where(pair == other, _alibi_slope(group * ATT_HEADS_PER_GROUP + other * heads + h), slope)
            s = _bdot_nt(q[:, cs], kk[:, cs]) * (ATT_HEAD_DIM ** -0.5) - slope * dist
            s = jnp.where(valid, s, NEG_INF)
            m = jnp.max(s, axis=-1, keepdims=True)
            p = jnp.exp(s - m)
            l = jnp.sum(p, axis=-1, keepdims=True)
            outs.append(_bdot(p, vv[:, cs]) / l)
            lses.append(jnp.broadcast_to(m + jnp.log(l), (blk, ATT_HEAD_DIM)))
        o_ref[0, rows, :] = jnp.concatenate(outs, axis=1)
        lse_ref[0, rows, :] = jnp.concatenate(lses, axis=1)

    per_trip = min(dil, ATT_BLOCKS_PER_TRIP)
    for sp in range(n_span):
        def trip(it, carry, sp=sp):
            for k in range(per_trip):
                block(sp, it * per_trip + k)
            return carry
        if dil == per_trip:
            trip(0, 0)
        else:
            lax.fori_loop(0, dil // per_trip, trip, 0)


def _attn_prompt(proj, group):
    window, dil = ATT_GROUPS[group]
    b, s, _ = proj.shape
    gw = ATT_GROUP_WIDTH
    span = dil * ATT_BLOCK
    tokens = max(span, min(s, ATT_TOKENS_PER_STEP))
    assert s % tokens == 0 and tokens % span == 0

    pairs = gw // LANES

    def spec(col, prev):
        if prev:
            return pl.BlockSpec((1, tokens, LANES), lambda bb, n, hp: (bb, jnp.maximum(n - 1, 0), (col + group) * pairs + hp))
        return pl.BlockSpec((1, tokens, LANES), lambda bb, n, hp: (bb, n, (col + group) * pairs + hp))

    out_spec = pl.BlockSpec((1, tokens, LANES), lambda bb, n, hp: (bb, n, hp))
    return pl.pallas_call(
        functools.partial(_attn_prompt_kernel, dil=dil, n_taps=window // dil, group=group),
        grid=(b, s // tokens, pairs),
        in_specs=[spec(COL_Q, False), spec(COL_K, True), spec(COL_K, False), spec(COL_V, True), spec(COL_V, False)],
        out_specs=[out_spec, out_spec],
        out_shape=[jax.ShapeDtypeStruct((b, s, gw), F32)] * 2,
        compiler_params=_params("arbitrary", "arbitrary", "arbitrary"),
        name=f"attn_prompt_g{group}",
    )(proj, proj, proj, proj, proj)


def _attn_sample_seq(sq, q_ref, kn_ref, vn_ref, ck_ref, cv_ref, o_ref, lse_ref, nk_ref, nv_ref, *, window, dil, group):
    t_new = q_ref.shape[1]
    w = ck_ref.shape[2]
    gw = ATT_GROUP_WIDTH
    nh = ATT_HEADS_PER_GROUP
    rows = nh * t_new
    q, kn, vn = q_ref[sq], kn_ref[sq], vn_ref[sq]
    ck, cv = ck_ref[sq], cv_ref[sq]

    r = lax.broadcasted_iota(jnp.int32, (rows, gw), 0)
    c = lax.broadcasted_iota(jnp.int32, (rows, gw), 1)
    head_mask = _div(r, t_new) == _div(c, ATT_HEAD_DIM)
    qbd = jnp.where(head_mask, jnp.concatenate([q] * nh, axis=0), 0.0)
    s_c = _bdot(qbd, ck) * (ATT_HEAD_DIM ** -0.5)
    s_n = _bdot_nt(qbd, kn) * (ATT_HEAD_DIM ** -0.5)

    hrow = _div(lax.broadcasted_iota(jnp.int32, (rows, 1), 0), t_new)
    slope = jnp.zeros((rows, 1), F32)
    for h in range(nh):
        slope = jnp.where(hrow == h, _alibi_slope(group * nh + h), slope)

    def masked(s, first_pos):
        tq = _mod(lax.broadcasted_iota(jnp.int32, s.shape, 0), t_new)
        dist = w + tq - (first_pos + lax.broadcasted_iota(jnp.int32, s.shape, 1))
        valid = (dist >= 0) & (_mod(dist, dil) == 0) & (dist <= window)
        return jnp.where(valid, s - slope * dist.astype(F32), NEG_INF)

    s_c = masked(s_c, 0)
    s_n = masked(s_n, w)
    m = jnp.maximum(jnp.max(s_c, axis=-1, keepdims=True), jnp.max(s_n, axis=-1, keepdims=True))
    p_c = jnp.exp(s_c - m)
    p_n = jnp.exp(s_n - m)
    l = jnp.sum(p_c, axis=-1, keepdims=True) + jnp.sum(p_n, axis=-1, keepdims=True)
    o_full = jnp.where(head_mask, (_bdot_nt(p_c, cv) + _bdot(p_n, vn)) / l, 0.0)
    lse_full = jnp.where(head_mask, m + jnp.log(l), 0.0)
    o, lse = o_full[0:t_new], lse_full[0:t_new]
    for h in range(1, nh):
        o = o + o_full[h * t_new:(h + 1) * t_new]
        lse = lse + lse_full[h * t_new:(h + 1) * t_new]
    o_ref[sq] = o
    lse_ref[sq] = lse

    lane = lax.broadcasted_iota(jnp.int32, (gw, LANES), 1)
    for new, old, out_ref in ((kn, ck, nk_ref), (vn, cv, nv_ref)):
        shifted = pltpu.roll(old, w - t_new, 1)
        new_t = pltpu.roll(_pad_rows(new, LANES).T, LANES - t_new, 1)
        if w > LANES:
            out_ref[sq, :, 0:w - LANES] = shifted[:, 0:w - LANES]
        out_ref[sq, :, w - LANES:w] = jnp.where(lane >= LANES - t_new, new_t, shifted[:, w - LANES:w])


def _attn_sample_kernel(*refs, **static):
    for sq in range(refs[0].shape[0]):
        _attn_sample_seq(sq, *refs, **static)


def _attn_sample(proj, cache_k, cache_v, group):
    window, dil = ATT_GROUPS[group]
    bd, t_new, _ = proj.shape
    w = cache_k.shape[2]
    gw = ATT_GROUP_WIDTH
    assert w >= window and w % LANES == 0 and t_new % SUBLANES == 0 and t_new < LANES
    bs = ATT_SEQS_PER_STEP[group]
    while bd % bs:
        bs //= 2

    def pspec(col):
        return pl.BlockSpec((bs, t_new, gw), lambda bb: (bb, 0, col + group))

    cspec = pl.BlockSpec((bs, gw, w), lambda bb: (bb, 0, 0))
    ospec = pl.BlockSpec((bs, t_new, gw), lambda bb: (bb, 0, 0))
    return pl.pallas_call(
        functools.partial(_attn_sample_kernel, window=window, dil=dil, group=group),
        grid=(bd // bs,),
        in_specs=[pspec(COL_Q), pspec(COL_K), pspec(COL_V), cspec, cspec],
        out_specs=[ospec, ospec, cspec, cspec],
        out_shape=[jax.ShapeDtypeStruct((bd, t_new, gw), F32)] * 2 + [jax.ShapeDtypeStruct((bd, gw, w), F32)] * 2,
        compiler_params=_params("arbitrary"),
        name=f"attn_sample_g{group}",
    )(proj, proj, proj, cache_k, cache_v)


def _hgrn_levels(chunk):
    levels, m = [], HG_SUB
    while m < chunk:
        levels.append(m)
        m *= 2
    return tuple(levels)


def _hgrn_kernel(zf_ref, zi_ref, zq_ref, zg_ref, lbraw_ref, gn_ref, s0_ref, o_ref, sfin_ref, st_ref, *, layer):
    nseq, chunk = zf_ref.shape[0], zf_ref.shape[1]
    chains = [(sq, hh) for sq in range(nseq) for hh in range(HG_HEADS_PER_STEP)]
    levels = _hgrn_levels(chunk)
    cidx = pl.program_id(2)
    dk, dv = HG_DK, HG_DV

    @pl.when(cidx == 0)
    def _():
        for sq, hh in chains:
            st_ref[sq, hh] = s0_ref[sq, hh].T

    raw = lbraw_ref[...]
    e = jnp.exp(raw - jnp.max(raw, axis=0, keepdims=True))
    lb_all = jnp.sum(e[0:layer + 1], axis=0, keepdims=True) / jnp.sum(e, axis=0, keepdims=True)

    rr = lax.broadcasted_iota(jnp.int32, (chunk, chunk), 0)
    cc = lax.broadcasted_iota(jnp.int32, (chunk, chunk), 1)
    sels = [cc <= rr] + [cc <= _div(rr, 2 * m) * (2 * m) + m - 1 for m in levels]
    sel = jnp.concatenate([jnp.where(sm, 1.0, 0.0) for sm in sels], axis=0).astype(BF16)

    trow = lax.broadcasted_iota(jnp.int32, (chunk, dk), 0)
    for sq, hh in chains:
        cs = slice(hh * dk, (hh + 1) * dk)
        lb = lb_all[:, cs]
        f = lb + (1.0 - lb) * jax.nn.sigmoid(zf_ref[sq][:, cs])
        lf = jnp.log(f)
        kh = 1.0 - f
        q = zq_ref[sq][:, cs]
        v = zi_ref[sq][:, cs]

        hi = lf.astype(BF16)
        r1 = lf - hi.astype(F32)
        mid = r1.astype(BF16)
        lo = (r1 - mid.astype(F32)).astype(BF16)
        bb = jnp.dot(sel, jnp.concatenate([hi, mid, lo], axis=1), preferred_element_type=F32)
        bb = bb[:, 0:dk] + bb[:, dk:2 * dk] + bb[:, 2 * dk:3 * dk]
        b = bb[0:chunk]

        st = st_ref[sq, hh]
        o = _bdot_nt(q * jnp.exp(b), st)

        for off in range(HG_SUB):
            if off == 0:
                a = jnp.sum(q * kh, axis=-1, keepdims=True)
                o = o + a * v
            else:
                in_block = _mod(trow, HG_SUB) >= off
                dec = jnp.exp(jnp.where(in_block, b - pltpu.roll(b, off, 0), NEG_INF))
                a = jnp.sum(q * pltpu.roll(kh, off, 0) * dec, axis=-1, keepdims=True)
                o = o + a * pltpu.roll(v, off, 0)

        if levels:
            ti = lax.broadcasted_iota(jnp.int32, (chunk, chunk), 0)
            si = lax.broadcasted_iota(jnp.int32, (chunk, chunk), 1)
            amat = jnp.zeros((chunk, chunk), F32)
            for li, m in enumerate(levels):
                bm = bb[(li + 1) * chunk:(li + 2) * chunk]
                second = _mod(_div(trow, m), 2) == 1
                qs = q * jnp.exp(jnp.where(second, b - bm, NEG_INF))
                ks = kh * jnp.exp(jnp.where(second, NEG_INF, bm - b))
                amat = amat + jnp.where(_div(ti, 2 * m) == _div(si, 2 * m), _bdot_nt(qs, ks), 0.0)
            o = o + _bdot(amat, v)

        b_end = b[chunk - 1:chunk]
        kd = kh * jnp.exp(b_end - b)
        vt = _pad_rows(v, dv).T
        st_ref[sq, hh] = st * jnp.exp(b_end) + _bdot(vt, _pad_rows(kd, dv))

        zg = zg_ref[sq][:, cs]
        o_ref[sq, :, cs] = _rms(o) * gn_ref[...] * (zg * jax.nn.sigmoid(zg))

    @pl.when(cidx == pl.num_programs(2) - 1)
    def _():
        for sq, hh in chains:
            sfin_ref[sq, hh] = st_ref[sq, hh].T


def _hgrn(proj, hg_lower_bound, hg_norm_g, s0, layer):
    b, l, _ = proj.shape
    assert HG_DK == HG_DV and HG_DV == LANES
    chunk = min(HG_CHUNK, l)
    assert l % chunk == 0 and chunk % HG_SUB == 0
    hps = HG_HEADS_PER_STEP
    wblk = hps * HG_DK
    bs = HG_SEQS_PER_STEP
    while b % bs:
        bs //= 2

    def pspec(col):
        return pl.BlockSpec((bs, chunk, wblk), lambda bb, hp, c: (bb, c, col + hp))

    sspec = pl.BlockSpec((bs, hps, HG_DK, HG_DV), lambda bb, hp, c: (bb, hp, 0, 0))
    return pl.pallas_call(
        functools.partial(_hgrn_kernel, layer=layer),
        grid=(b // bs, HG_HEADS // hps, l // chunk),
        in_specs=[pspec(COL_ZF), pspec(COL_ZI), pspec(COL_ZQ), pspec(COL_ZG),
                  pl.BlockSpec((hg_lower_bound.shape[0], wblk), lambda bb, hp, c: (0, hp)),
                  pl.BlockSpec((1, HG_DV), lambda bb, hp, c: (0, 0)),
                  sspec],
        out_specs=[pl.BlockSpec((bs, chunk, wblk), lambda bb, hp, c: (bb, c, hp)), sspec],
        out_shape=[jax.ShapeDtypeStruct((b, l, HG_HEADS * HG_DV), F32),
                   jax.ShapeDtypeStruct(s0.shape, F32)],
        scratch_shapes=[pltpu.VMEM((bs, hps, HG_DV, HG_DK), F32)],
        compiler_params=_params("arbitrary", "arbitrary", "arbitrary"),
        name="hgrn2",
    )(proj, proj, proj, proj, hg_lower_bound, hg_norm_g.reshape(1, HG_DV), s0)


def _merge_kernel(x_ref, o0_ref, l0_ref, o1_ref, l1_ref, o2_ref, l2_ref, oh_ref, za_ref, zh_ref,
                  g1_ref, sh2_ref, sc2_ref, n2_ref, wa_ref, wh_ref, wo_ref, x1_ref, h2_ref):
    nb, lb, d = x_ref.shape
    tm = nb * lb

    def flat(ref):
        return ref[...].reshape(tm, ref.shape[-1])

    l0, l1, l2 = flat(l0_ref), flat(l1_ref), flat(l2_ref)
    m = jnp.maximum(jnp.maximum(l0, l1), l2)
    e0, e1, e2 = jnp.exp(l0 - m), jnp.exp(l1 - m), jnp.exp(l2 - m)
    att = (e0 * flat(o0_ref) + e1 * flat(o1_ref) + e2 * flat(o2_ref)) / (e0 + e1 + e2)
    merged = (jax.nn.sigmoid(flat(za_ref)) * _bdot(att, wa_ref[...])
              + jax.nn.sigmoid(flat(zh_ref)) * _bdot(flat(oh_ref), wh_ref[...]))
    mix = _bdot(merged, wo_ref[...])
    x1 = x_ref[...] + g1_ref[...] * mix.reshape(nb, lb, d)
    x1_ref[...] = x1
    h2 = _rms(x1) * n2_ref[...] * (1.0 + sc2_ref[...]) + sh2_ref[...]
    h2_ref[...] = h2.reshape(tm, d).astype(BF16)


def _merge(x, att_o, att_lse, o_h, proj, mod3, norm2_g, wa, wh, wo):
    n, l, d = x.shape
    nb, lb = _token_blocks(n, l)
    gw = ATT_GROUP_WIDTH
    tpb = l // lb

    def tok(width, col=0):
        return pl.BlockSpec((nb, lb, width), lambda i, t: (i, t, col))

    def modspec(col):
        return pl.BlockSpec((nb, 1, d), lambda i, t: (i, 0, col))

    def full(a):
        return pl.BlockSpec(a.shape, lambda i, t: (0, 0))

    att_specs, att_args = [], []
    for o, lse in zip(att_o, att_lse):
        att_specs += [tok(gw), tok(gw)]
        att_args += [o, lse]
    return pl.pallas_call(
        _merge_kernel,
        grid=(n // nb, tpb),
        in_specs=[tok(d)] + att_specs + [tok(o_h.shape[-1]), tok(d, COL_ZA), tok(d, COL_ZH),
                                         modspec(2), modspec(3), modspec(4),
                                         pl.BlockSpec((1, d), lambda i, t: (0, 0)), full(wa), full(wh), full(wo)],
        out_specs=[tok(d), pl.BlockSpec((nb * lb, d), lambda i, t: (i * tpb + t, 0))],
        out_shape=[jax.ShapeDtypeStruct((n, l, d), F32), jax.ShapeDtypeStruct((n * l, d), BF16)],
        compiler_params=_params("arbitrary", "arbitrary"),
        name="merge",
    )(x, *att_args, o_h, proj, proj, mod3, mod3, mod3, norm2_g.reshape(1, d), wa, wh, wo)


def _top16(arrays):
    krow = lax.broadcasted_iota(jnp.int32, (PEER_TOPK, LANES), 0)

    def body(i, carry):
        out = []
        for s, vals, idxs in carry:
            rows = s.shape[0]
            row = lax.broadcasted_iota(jnp.int32, s.shape, 0)
            m = jnp.max(s, axis=0, keepdims=True)
            idx = jnp.min(jnp.where(s == m, row, rows), axis=0, keepdims=True)
            s = jnp.where(row == idx, NEG_INF, s)
            vals = jnp.where(krow == i, m, vals)
            idxs = jnp.where(krow == i, idx, idxs)
            out.append((s, vals, idxs))
        return tuple(out)

    init = tuple((s, jnp.zeros((PEER_TOPK, LANES), F32), jnp.zeros((PEER_TOPK, LANES), jnp.int32)) for s in arrays)
    return [(vals, idxs) for _, vals, idxs in lax.fori_loop(0, PEER_TOPK, body, init)]


def _bitonic_merge_desc(v):
    n = len(v)
    v = list(v)
    j = n // 2
    while j >= 1:
        for i in range(n):
            if i & j == 0:
                v[i], v[i + j] = jnp.maximum(v[i], v[i + j]), jnp.minimum(v[i], v[i + j])
        j //= 2
    return v


def _top16_values(s):
    kk = PEER_TOPK
    rows = s.shape[0]
    assert rows % SUBLANES == 0 and rows <= kk * SUBLANES
    v = [s[SUBLANES * i:SUBLANES * (i + 1)] for i in range(rows // SUBLANES)]
    v += [jnp.full((SUBLANES, LANES), NEG_INF, F32)] * (kk - len(v))
    size = 2
    while size <= kk:
        j = size // 2
        while j >= 1:
            for i in range(kk):
                if i & j == 0:
                    hi, lo = jnp.maximum(v[i], v[i + j]), jnp.minimum(v[i], v[i + j])
                    v[i], v[i + j] = (hi, lo) if i & size == 0 else (lo, hi)
            j //= 2
        size *= 2
    for shift in (1, 2, 4):
        other = [pltpu.roll(x, SUBLANES - shift, 0) for x in v]
        v = _bitonic_merge_desc([jnp.maximum(v[i], other[kk - 1 - i]) for i in range(kk)])
    return jnp.concatenate([x[0:1] for x in v], axis=0)


def _count_ge(s, threshold):
    return jnp.sum(jnp.where(s >= threshold, 1.0, 0.0), axis=0, keepdims=True)


def _peer_select_kernel(h2_ref, wq_ref, sk_ref, r2_ref, t1_ref, a1_ref, a2_ref, s_ref):
    tm = h2_ref.shape[0]
    kk = PEER_TOPK
    nk = PEER_NKEYS
    qry = jnp.dot(h2_ref[...], wq_ref[...], preferred_element_type=F32).astype(BF16)
    for h in range(PEER_HEADS):
        for p in range(2):
            c0 = (2 * h + p) * PEER_DHALF
            st = _bdot_nt(sk_ref[h, p], qry[:, c0:c0 + PEER_DHALF])
            for c in range(tm // LANES):
                s_ref[2 * h + p, c] = st[:, c * LANES:(c + 1) * LANES]

    row = lax.broadcasted_iota(jnp.int32, (nk, LANES), 0)
    krow = lax.broadcasted_iota(jnp.int32, (kk, LANES), 0)

    def candidates(v1, v2):
        return jnp.concatenate([v1[0:1] + v2] + [v1[a:a + 1] + v2[0:SUBLANES] for a in range(1, SUBLANES)]
                               + [v1[SUBLANES:kk] + v2[0:1]], axis=0)

    def finish(h, c, v1, i1, v2, i2, sf, ci):
        first = jnp.where(ci < kk, 0, jnp.where(ci < kk + 7 * SUBLANES, 1 + _div(ci - kk, SUBLANES),
                                                ci - (kk + 7 * SUBLANES) + SUBLANES))
        z = jnp.sum(jnp.exp(sf - sf[0:1]), axis=0, keepdims=True)
        count = jnp.zeros((kk, LANES), F32)
        for k in range(kk):
            count = count + jnp.where(krow == first[k:k + 1], 1.0, 0.0)
        t1 = jnp.zeros((nk, LANES), F32)
        r2 = jnp.full((nk, LANES), float(kk), F32)
        for a in range(kk):
            t1 = jnp.where(row == i1[a:a + 1], count[a:a + 1], t1)
            r2 = jnp.where(row == i2[a:a + 1], float(a), r2)
        r2_ref[h, c] = r2
        t1_ref[h, c] = t1
        a1_ref[h, c] = jnp.exp(s_ref[2 * h, c] - v1[0:1]) / z
        a2_ref[h, c] = jnp.exp(s_ref[2 * h + 1, c] - v2[0:1])

    def exact_pair(c, h0):
        halves = []
        for h in (h0, h0 + 1):
            (v1, i1), (v2, i2) = _top16([s_ref[2 * h, c], s_ref[2 * h + 1, c]])
            halves.append((v1, i1, v2, i2))
        picks = _top16([candidates(v1, v2) for v1, _, v2, _ in halves])
        for h, half, (sf, ci) in zip((h0, h0 + 1), halves, picks):
            finish(h, c, *half, sf, ci)

    def distinct_pair(c, h0):
        heads = (h0, h0 + 1)
        scores = [(s_ref[2 * h, c], s_ref[2 * h + 1, c]) for h in heads]
        tops = [(_top16_values(s1), _top16_values(s2)) for s1, s2 in scores]
        cands = [candidates(v1, v2) for v1, v2 in tops]
        sums = [_top16_values(cand) for cand in cands]
        doubtful = jnp.zeros((1, LANES), F32)
        for h, (s1, s2), (v1, v2), cand, sf in zip(heads, scores, tops, cands, sums):
            picked = jnp.where(cand >= sf[kk - 1:kk], 1.0, 0.0)
            repeats = sum(jnp.sum(jnp.where(v[0:kk - 1] == v[1:kk], 1.0, 0.0), axis=0, keepdims=True)
                          for v in (v1, v2, sf))
            sure = ((_count_ge(s1, v1[kk - 1:kk]) == kk) & (_count_ge(s2, v2[kk - 1:kk]) == kk)
                    & (jnp.sum(picked, axis=0, keepdims=True) == kk) & (repeats == 0.0))
            doubtful = doubtful + jnp.where(sure, 0.0, 1.0)
            count = jnp.concatenate(
                [jnp.sum(picked[0:kk], axis=0, keepdims=True)]
                + [jnp.sum(picked[kk + (a - 1) * SUBLANES:kk + a * SUBLANES], axis=0, keepdims=True)
                   for a in range(1, SUBLANES)]
                + [picked[kk + 7 * SUBLANES:]], axis=0)
            z = jnp.sum(jnp.exp(sf - sf[0:1]), axis=0, keepdims=True)
            t1 = jnp.zeros((nk, LANES), F32)
            r2 = jnp.full((nk, LANES), float(kk), F32)
            for a in range(kk):
                t1 = jnp.where(s1 == v1[a:a + 1], count[a:a + 1], t1)
                r2 = jnp.where(s2 == v2[a:a + 1], float(a), r2)
            r2_ref[h, c] = r2
            t1_ref[h, c] = t1
            a1_ref[h, c] = jnp.exp(s1 - v1[0:1]) / z
            a2_ref[h, c] = jnp.exp(s2 - v2[0:1])
        return jnp.sum(doubtful)

    def chunk_body(c, carry):
        for h0 in range(0, PEER_HEADS, 2):
            doubtful = distinct_pair(c, h0)

            @pl.when(doubtful > 0.0)
            def _():
                exact_pair(c, h0)
        return carry

    lax.fori_loop(0, tm // LANES, chunk_body, 0)


def _peer_select(h2, wq, subkeys):
    t, d = h2.shape
    tm = min(t, 512)
    assert t % tm == 0 and tm % LANES == 0
    nch = tm // LANES
    ospec = pl.BlockSpec((PEER_HEADS, nch, PEER_NKEYS, LANES), lambda i: (0, i, 0, 0))

    def shape(dtype):
        return jax.ShapeDtypeStruct((PEER_HEADS, t // LANES, PEER_NKEYS, LANES), dtype)

    return pl.pallas_call(
        _peer_select_kernel,
        grid=(t // tm,),
        in_specs=[pl.BlockSpec((tm, d), lambda i: (i, 0)),
                  pl.BlockSpec(wq.shape, lambda i: (0, 0)),
                  pl.BlockSpec(subkeys.shape, lambda i: (0, 0, 0, 0))],
        out_specs=[ospec] * 4,
        out_shape=[shape(F32)] * 4,
        scratch_shapes=[pltpu.VMEM((2 * PEER_HEADS, nch, PEER_NKEYS, LANES), F32)],
        compiler_params=_params("arbitrary"),
        name="peer_select",
    )(h2, wq, subkeys)


def _gelu_tanh(x):
    return x * (0.5 * (1.0 + jnp.tanh(0.7978845608028654 * (x + 0.044715 * (x * x * x)))))


def _peer_apply_kernel(h2_ref, u_ref, vt_ref, r2_ref, t1_ref, a1_ref, a2_ref, x1_ref, g2_ref, fn_ref,
                       y_ref, acc_ref, act_ref, gw_ref, r2b_ref, a2b_ref):
    nb, lb, d = x1_ref.shape
    tm = nb * lb
    nch = tm // LANES
    nk = PEER_NKEYS
    ne1 = PEER_E1_PER_STEP
    groups = nk // BF16_ROWS
    j = pl.program_id(2)

    @pl.when(j == 0)
    def _():
        acc_ref[...] = jnp.zeros_like(acc_ref)
        for h in range(PEER_HEADS):
            for tc in range(tm // LANES):
                r2b_ref[h, tc] = r2_ref[h, tc].astype(BF16)
                a2b_ref[h, tc] = a2_ref[h, tc].astype(BF16)

    act_ref[...] = _gelu_tanh(lax.dot_general(u_ref[...], h2_ref[...], (((1,), (1,)), ((), ())),
                                              preferred_element_type=F32)).astype(BF16)

    def bcast(tile, e):
        return jnp.broadcast_to(tile[e:e + 1], (BF16_ROWS, LANES)).astype(BF16)

    zero = jnp.zeros((BF16_ROWS, LANES), BF16)
    for tc in range(nch):
        ls = slice(tc * LANES, (tc + 1) * LANES)
        for e in range(ne1):
            wsum = [zero] * groups
            for h in range(PEER_HEADS):
                t1 = bcast(t1_ref[h, tc], e)
                a1 = bcast(a1_ref[h, tc], e)
                for g in range(groups):
                    gs = slice(g * BF16_ROWS, (g + 1) * BF16_ROWS)
                    wsum[g] = wsum[g] + jnp.where(r2b_ref[h, tc, gs, :] < t1, a1 * a2b_ref[h, tc, gs, :], zero)
            for g in range(groups):
                rs = slice(e * nk + g * BF16_ROWS, e * nk + (g + 1) * BF16_ROWS)
                gw_ref[rs, ls] = wsum[g] * act_ref[rs, ls]
    acc_ref[...] += jnp.dot(vt_ref[...], gw_ref[...], preferred_element_type=F32)

    @pl.when(j == pl.num_programs(2) - 1)
    def _():
        ff = acc_ref[...].T.reshape(nb, lb, d)
        y_ref[...] = _rms(x1_ref[...] + g2_ref[...] * ff) * fn_ref[...]


def _peer_apply(h2, u_bf16, vt_bf16, sel, x1, mod3, final_g):
    n, l, d = x1.shape
    nb, lb = _token_blocks(n, l)
    tm = nb * lb
    tpb = l // lb
    ne1 = PEER_E1_PER_STEP
    assert ne1 % SUBLANES == 0 and tm % LANES == 0
    te = ne1 * PEER_NKEYS
    nch = tm // LANES
    selspec = pl.BlockSpec((PEER_HEADS, nch, PEER_NKEYS, LANES), lambda i, t, j: (0, i * tpb + t, 0, 0))
    rowspec = pl.BlockSpec((PEER_HEADS, nch, ne1, LANES), lambda i, t, j: (0, i * tpb + t, j, 0))
    tok = pl.BlockSpec((nb, lb, d), lambda i, t, j: (i, t, 0))
    return pl.pallas_call(
        _peer_apply_kernel,
        grid=(n // nb, tpb, PEER_NKEYS // ne1),
        in_specs=[pl.BlockSpec((tm, d), lambda i, t, j: (i * tpb + t, 0)),
                  pl.BlockSpec((te, d), lambda i, t, j: (j, 0)),
                  pl.BlockSpec((d, te), lambda i, t, j: (0, j)),
                  selspec, rowspec, rowspec, selspec,
                  tok,
                  pl.BlockSpec((nb, 1, d), lambda i, t, j: (i, 0, 5)),
                  pl.BlockSpec((1, d), lambda i, t, j: (0, 0))],
        out_specs=tok,
        out_shape=jax.ShapeDtypeStruct((n, l, d), F32),
        scratch_shapes=[pltpu.VMEM((d, tm), F32),
                        pltpu.VMEM((te, tm), BF16), pltpu.VMEM((te, tm), BF16),
                        pltpu.VMEM((PEER_HEADS, nch, PEER_NKEYS, LANES), BF16),
                        pltpu.VMEM((PEER_HEADS, nch, PEER_NKEYS, LANES), BF16)],
        compiler_params=_params("arbitrary", "arbitrary", "arbitrary"),
        name="peer_apply",
    )(h2, u_bf16, vt_bf16, *sel, x1, mod3, final_g.reshape(1, d))


def _to_rows(buf):
    n, _, w = buf.shape
    return buf.reshape(n, ATT_HEADS_PER_GROUP, ATT_HEAD_DIM, w).transpose(0, 3, 1, 2)


def _to_cols(buf):
    n, w = buf.shape[0], buf.shape[1]
    return buf.transpose(0, 2, 3, 1).reshape(n, ATT_GROUP_WIDTH, w)


def _layer(x, mod, w, caches, s0, layer, final_g):
    n, l, d = x.shape
    mod3 = mod.reshape(n, 1, N_MOD * d)
    proj = _inproj(x, mod3, w["norm1_g"], w["w_in"])
    att_o, att_lse, bufs = [], [], []
    for g, (window, _) in enumerate(ATT_GROUPS):
        if caches is None:
            o, lse = _attn_prompt(proj, g)
            keep = min(window, l)
            k0 = (COL_K + g) * ATT_GROUP_WIDTH
            v0 = (COL_V + g) * ATT_GROUP_WIDTH
            kb = proj[:, l - keep:, k0:k0 + ATT_GROUP_WIDTH].reshape(n, keep, ATT_HEADS_PER_GROUP, ATT_HEAD_DIM)
            vb = proj[:, l - keep:, v0:v0 + ATT_GROUP_WIDTH].reshape(n, keep, ATT_HEADS_PER_GROUP, ATT_HEAD_DIM)
        else:
            o, lse, kb, vb = _attn_sample(proj, _to_cols(caches[2 * g]), _to_cols(caches[2 * g + 1]), g)
            kb, vb = _to_rows(kb), _to_rows(vb)
        att_o.append(o)
        att_lse.append(lse)
        bufs += [kb, vb]
    o_h, s_fin = _hgrn(proj, w["hg_lower_bound"], w["hg_norm_g"], s0, layer)
    x1, h2 = _merge(x, att_o, att_lse, o_h, proj, mod3, w["norm2_g"], w["w_att_branch"], w["w_hg_branch"], w["w_out"])
    sel = _peer_select(h2, w["peer_wq"], w["peer_subkeys"])
    y = _peer_apply(h2, w["peer_u"], w["peer_vt"], sel, x1, mod3, final_g)
    return y, bufs, s_fin


def kernel(x_prompt, x_sample, cache_k0, cache_v0, cache_k1, cache_v1, cache_k2, cache_v2, state_hgrn, c_prompt, c_sample, ada_w, ada_b, norm1_g, w_in, w_att_branch, w_hg_branch, w_out, hg_lower_bound, hg_norm_g, norm2_g, peer_wq, peer_subkeys, peer_u, peer_v, final_norm_g):
    depth = ada_w.shape[0]
    assert depth == 1, "the final RMSNorm is fused into the last PEER kernel of a single layer"
    n_p = x_prompt.shape[0]
    layer = 0
    w = {
        "norm1_g": norm1_g[layer],
        "w_in": jnp.concatenate([w_in[layer][:, ORIG_GATE_START:], w_in[layer][:, :ORIG_GATE_START]], axis=1).astype(BF16),
        "w_att_branch": w_att_branch[layer].astype(BF16),
        "w_hg_branch": w_hg_branch[layer].astype(BF16),
        "w_out": w_out[layer].astype(BF16),
        "hg_lower_bound": hg_lower_bound,
        "hg_norm_g": hg_norm_g[layer],
        "norm2_g": norm2_g[layer],
        "peer_wq": peer_wq[layer].astype(BF16),
        "peer_subkeys": peer_subkeys[layer].astype(BF16),
        "peer_u": peer_u[layer].astype(BF16),
        "peer_vt": peer_v[layer].astype(BF16).T,
    }
    mod = _adaln(jnp.concatenate([c_prompt, c_sample], axis=0), ada_w[layer], ada_b[layer])
    s0_p = jnp.zeros((n_p,) + state_hgrn.shape[2:], F32)
    y_p, bufs_p, st_p = _layer(x_prompt, mod[:n_p], w, None, s0_p, layer, final_norm_g)
    caches = (cache_k0[layer], cache_v0[layer], cache_k1[layer], cache_v1[layer], cache_k2[layer], cache_v2[layer])
    y_s, bufs_s, st_s = _layer(x_sample, mod[n_p:], w, caches, state_hgrn[layer], layer, final_norm_g)
    return (y_p, y_s, *[b[None] for b in bufs_p], st_p[None], *[b[None] for b in bufs_s], st_s[None])
```

```python
import functools

import jax
import jax.numpy as jnp
from jax import lax
from jax.experimental import pallas as pl
from jax.experimental.pallas import tpu as pltpu

F32 = jnp.float32
BF16 = jnp.bfloat16
NEG_INF = float("-inf")
LANES = 128
SUBLANES = 8
BF16_ROWS = 16

NORM_EPS = 1e-6
N_MOD = 6
INPROJ_TOKENS = 1024
INPROJ_COLS = 1280

ATT_HEAD_DIM = 64
ATT_GROUPS = ((128, 1), (512, 4), (2048, 16))
ATT_HEADS_PER_GROUP = 4
ATT_HEADS = len(ATT_GROUPS) * ATT_HEADS_PER_GROUP
ATT_GROUP_WIDTH = ATT_HEADS_PER_GROUP * ATT_HEAD_DIM
ATT_BLOCK = 128
ATT_TOKENS_PER_STEP = 512
ATT_BLOCKS_PER_TRIP = 4
ATT_SEQS_PER_STEP = (8, 4, 1)
ALIBI_MAX_EXP = 8.0

HG_HEADS = 4
HG_DK = 128
HG_DV = 128
HG_CHUNK = 128
HG_SUB = SUBLANES
HG_HEADS_PER_STEP = 2
HG_SEQS_PER_STEP = 8

PEER_HEADS = 8
PEER_NKEYS = 128
PEER_DHALF = 128
PEER_TOPK = 16
PEER_ACT_ROWS = 512
PEER_E1_PER_STEP = 16

COL_ZA, COL_ZH = 0, 1
COL_Q, COL_K, COL_V = 8, 11, 14
COL_ZF, COL_ZI, COL_ZQ, COL_ZG = 17, 19, 21, 23
ORIG_GATE_START = 4352

VMEM_LIMIT = 56 * 1024 * 1024


def _bdot(a, b):
    return jnp.dot(a.astype(BF16), b.astype(BF16), preferred_element_type=F32)


def _bdot_nt(a, b):
    return lax.dot_general(a.astype(BF16), b.astype(BF16), (((1,), (1,)), ((), ())),
                           preferred_element_type=F32)


def _log2(k):
    assert k > 0 and k & (k - 1) == 0, k
    return k.bit_length() - 1


def _div(x, k):
    return x >> _log2(k)


def _mod(x, k):
    return x & ((1 << _log2(k)) - 1)


def _rms(x):
    return x * lax.rsqrt(jnp.mean(x * x, axis=-1, keepdims=True) + NORM_EPS)


def _pad_rows(a, rows):
    if a.shape[0] == rows:
        return a
    return jnp.concatenate([a, jnp.zeros((rows - a.shape[0], a.shape[1]), a.dtype)], axis=0)


def _params(*sem):
    return pltpu.CompilerParams(dimension_semantics=sem, vmem_limit_bytes=VMEM_LIMIT)


def _adaln_kernel(c_ref, w_ref, b_ref, o_ref):
    c = c_ref[...]
    o_ref[...] = _bdot(c * jax.nn.sigmoid(c), w_ref[...]) + b_ref[...]


def _adaln(c, ada_w, ada_b):
    n, d = c.shape
    cols = ada_w.shape[1]
    tn = 1536
    return pl.pallas_call(
        _adaln_kernel,
        grid=(cols // tn,),
        in_specs=[pl.BlockSpec((n, d), lambda j: (0, 0)),
                  pl.BlockSpec((d, tn), lambda j: (0, j)),
                  pl.BlockSpec((1, tn), lambda j: (0, j))],
        out_specs=pl.BlockSpec((n, tn), lambda j: (0, j)),
        out_shape=jax.ShapeDtypeStruct((n, cols), F32),
        compiler_params=_params("arbitrary"),
        name="adaln",
    )(c, ada_w, ada_b.reshape(1, cols))


def _inproj_kernel(x_ref, sh_ref, sc_ref, g_ref, w_ref, o_ref, h_ref):
    nb, lb, d = x_ref.shape

    @pl.when(pl.program_id(2) == 0)
    def _():
        h = _rms(x_ref[...]) * g_ref[...] * (1.0 + sc_ref[...]) + sh_ref[...]
        h_ref[...] = h.reshape(nb * lb, d).astype(BF16)

    o = jnp.dot(h_ref[...], w_ref[...], preferred_element_type=F32)
    o_ref[...] = o.reshape(nb, lb, o.shape[-1])


def _token_blocks(n, l, tokens=512):
    if l >= tokens:
        return 1, tokens
    return min(n, tokens // l), l


def _inproj(x, mod3, norm_g, w_bf16):
    n, l, d = x.shape
    cols = w_bf16.shape[1]
    nb, lb = _token_blocks(n, l, INPROJ_TOKENS)
    tn = INPROJ_COLS
    return pl.pallas_call(
        _inproj_kernel,
        grid=(n // nb, l // lb, cols // tn),
        in_specs=[pl.BlockSpec((nb, lb, d), lambda i, t, j: (i, t, 0)),
                  pl.BlockSpec((nb, 1, d), lambda i, t, j: (i, 0, 0)),
                  pl.BlockSpec((nb, 1, d), lambda i, t, j: (i, 0, 1)),
                  pl.BlockSpec((1, d), lambda i, t, j: (0, 0)),
                  pl.BlockSpec((d, tn), lambda i, t, j: (0, j))],
        out_specs=pl.BlockSpec((nb, lb, tn), lambda i, t, j: (i, t, j)),
        out_shape=jax.ShapeDtypeStruct((n, l, cols), F32),
        scratch_shapes=[pltpu.VMEM((nb * lb, d), BF16)],
        compiler_params=_params("arbitrary", "arbitrary", "arbitrary"),
        name="inproj",
    )(x, mod3, mod3, norm_g.reshape(1, d), w_bf16)


def _alibi_slope(head):
    return 2.0 ** (-ALIBI_MAX_EXP * (head + 1) / ATT_HEADS)


def _attn_prompt_kernel(q_ref, kp_ref, kc_ref, vp_ref, vc_ref, o_ref, lse_ref, *, dil, n_taps, group):
    blk = ATT_BLOCK
    heads = LANES // ATT_HEAD_DIM
    span = blk * dil
    n_span = q_ref.shape[1] // span
    n = pl.program_id(1)
    pair = pl.program_id(2)
    i = lax.broadcasted_iota(jnp.int32, (blk, 2 * blk), 0)
    j = lax.broadcasted_iota(jnp.int32, (blk, 2 * blk), 1)
    delta = i + blk - j
    banded = (delta >= 0) & (delta <= n_taps)
    dist = (delta * dil).astype(F32)

    def rows_of(sp, r):
        return pl.ds(sp * span + r, blk, stride=dil) if dil > 1 else pl.ds(sp * span, blk)

    def block(sp, r):
        rows = rows_of(sp, r)
        q = q_ref[0, rows, :]
        if sp == 0:
            prev = rows_of(n_span - 1, r)
            k_prev, v_prev = kp_ref[0, prev, :], vp_ref[0, prev, :]
            valid = banded & ((j >= blk) | (n > 0))
        else:
            prev = rows_of(sp - 1, r)
            k_prev, v_prev = kc_ref[0, prev, :], vc_ref[0, prev, :]
            valid = banded
        kk = jnp.concatenate([k_prev, kc_ref[0, rows, :]], axis=0)
        vv = jnp.concatenate([v_prev, vc_ref[0, rows, :]], axis=0)
        outs, lses = [], []
        for h in range(heads):
            cs = slice(h * ATT_HEAD_DIM, (h + 1) * ATT_HEAD_DIM)
            slope = _alibi_slope(group * ATT_HEADS_PER_GROUP + h)
            for other in range(1, ATT_HEADS_PER_GROUP // heads):
                slope = jnp.where(pair == other, _alibi_slope(group * ATT_HEADS_PER_GROUP + other * heads + h), slope)
            s = _bdot_nt(q[:, cs], kk[:, cs]) * (ATT_HEAD_DIM ** -0.5) - slope * dist
            s = jnp.where(valid, s, NEG_INF)
            m = jnp.max(s, axis=-1, keepdims=True)
            p = jnp.exp(s - m)
            l = jnp.sum(p, axis=-1, keepdims=True)
            outs.append(_bdot(p, vv[:, cs]) / l)
            lses.append(jnp.broadcast_to(m + jnp.log(l), (blk, ATT_HEAD_DIM)))
        o_ref[0, rows, :] = jnp.concatenate(outs, axis=1)
        lse_ref[0, rows, :] = jnp.concatenate(lses, axis=1)

    per_trip = min(dil, ATT_BLOCKS_PER_TRIP)
    for sp in range(n_span):
        def trip(it, carry, sp=sp):
            for k in range(per_trip):
                block(sp, it * per_trip + k)
            return carry
        if dil == per_trip:
            trip(0, 0)
        else:
            lax.fori_loop(0, dil // per_trip, trip, 0)


def _attn_prompt(proj, group):
    window, dil = ATT_GROUPS[group]
    b, s, _ = proj.shape
    gw = ATT_GROUP_WIDTH
    span = dil * ATT_BLOCK
    tokens = max(span, min(s, ATT_TOKENS_PER_STEP))
    assert s % tokens == 0 and tokens % span == 0

    pairs = gw // LANES

    def spec(col, prev):
        if prev:
            return pl.BlockSpec((1, tokens, LANES), lambda bb, n, hp: (bb, jnp.maximum(n - 1, 0), (col + group) * pairs + hp))
        return pl.BlockSpec((1, tokens, LANES), lambda bb, n, hp: (bb, n, (col + group) * pairs + hp))

    out_spec = pl.BlockSpec((1, tokens, LANES), lambda bb, n, hp: (bb, n, hp))
    return pl.pallas_call(
        functools.partial(_attn_prompt_kernel, dil=dil, n_taps=window // dil, group=group),
        grid=(b, s // tokens, pairs),
        in_specs=[spec(COL_Q, False), spec(COL_K, True), spec(COL_K, False), spec(COL_V, True), spec(COL_V, False)],
        out_specs=[out_spec, out_spec],
        out_shape=[jax.ShapeDtypeStruct((b, s, gw), F32)] * 2,
        compiler_params=_params("arbitrary", "arbitrary", "arbitrary"),
        name=f"attn_prompt_g{group}",
    )(proj, proj, proj, proj, proj)


def _attn_sample_seq(sq, q_ref, kn_ref, vn_ref, ck_ref, cv_ref, o_ref, lse_ref, nk_ref, nv_ref, *, window, dil, group):
    t_new = q_ref.shape[1]
    w = ck_ref.shape[2]
    gw = ATT_GROUP_WIDTH
    nh = ATT_HEADS_PER_GROUP
    rows = nh * t_new
    q, kn, vn = q_ref[sq], kn_ref[sq], vn_ref[sq]
    ck, cv = ck_ref[sq], cv_ref[sq]

    r = lax.broadcasted_iota(jnp.int32, (rows, gw), 0)
    c = lax.broadcasted_iota(jnp.int32, (rows, gw), 1)
    head_mask = _div(r, t_new) == _div(c, ATT_HEAD_DIM)
    qbd = jnp.where(head_mask, jnp.concatenate([q] * nh, axis=0), 0.0)
    s_c = _bdot(qbd, ck) * (ATT_HEAD_DIM ** -0.5)
    s_n = _bdot_nt(qbd, kn) * (ATT_HEAD_DIM ** -0.5)

    hrow = _div(lax.broadcasted_iota(jnp.int32, (rows, 1), 0), t_new)
    slope = jnp.zeros((rows, 1), F32)
    for h in range(nh):
        slope = jnp.where(hrow == h, _alibi_slope(group * nh + h), slope)

    def masked(s, first_pos):
        tq = _mod(lax.broadcasted_iota(jnp.int32, s.shape, 0), t_new)
        dist = w + tq - (first_pos + lax.broadcasted_iota(jnp.int32, s.shape, 1))
        valid = (dist >= 0) & (_mod(dist, dil) == 0) & (dist <= window)
        return jnp.where(valid, s - slope * dist.astype(F32), NEG_INF)

    s_c = masked(s_c, 0)
    s_n = masked(s_n, w)
    m = jnp.maximum(jnp.max(s_c, axis=-1, keepdims=True), jnp.max(s_n, axis=-1, keepdims=True))
    p_c = jnp.exp(s_c - m)
    p_n = jnp.exp(s_n - m)
    l = jnp.sum(p_c, axis=-1, keepdims=True) + jnp.sum(p_n, axis=-1, keepdims=True)
    o_full = jnp.where(head_mask, (_bdot_nt(p_c, cv) + _bdot(p_n, vn)) / l, 0.0)
    lse_full = jnp.where(head_mask, m + jnp.log(l), 0.0)
    o, lse = o_full[0:t_new], lse_full[0:t_new]
    for h in range(1, nh):
        o = o + o_full[h * t_new:(h + 1) * t_new]
        lse = lse + lse_full[h * t_new:(h + 1) * t_new]
    o_ref[sq] = o
    lse_ref[sq] = lse

    lane = lax.broadcasted_iota(jnp.int32, (gw, LANES), 1)
    for new, old, out_ref in ((kn, ck, nk_ref), (vn, cv, nv_ref)):
        shifted = pltpu.roll(old, w - t_new, 1)
        new_t = pltpu.roll(_pad_rows(new, LANES).T, LANES - t_new, 1)
        if w > LANES:
            out_ref[sq, :, 0:w - LANES] = shifted[:, 0:w - LANES]
        out_ref[sq, :, w - LANES:w] = jnp.where(lane >= LANES - t_new, new_t, shifted[:, w - LANES:w])


def _attn_sample_kernel(*refs, **static):
    for sq in range(refs[0].shape[0]):
        _attn_sample_seq(sq, *refs, **static)


def _attn_sample(proj, cache_k, cache_v, group):
    window, dil = ATT_GROUPS[group]
    bd, t_new, _ = proj.shape
    w = cache_k.shape[2]
    gw = ATT_GROUP_WIDTH
    assert w >= window and w % LANES == 0 and t_new % SUBLANES == 0 and t_new < LANES
    bs = ATT_SEQS_PER_STEP[group]
    while bd % bs:
        bs //= 2

    def pspec(col):
        return pl.BlockSpec((bs, t_new, gw), lambda bb: (bb, 0, col + group))

    cspec = pl.BlockSpec((bs, gw, w), lambda bb: (bb, 0, 0))
    ospec = pl.BlockSpec((bs, t_new, gw), lambda bb: (bb, 0, 0))
    return pl.pallas_call(
        functools.partial(_attn_sample_kernel, window=window, dil=dil, group=group),
        grid=(bd // bs,),
        in_specs=[pspec(COL_Q), pspec(COL_K), pspec(COL_V), cspec, cspec],
        out_specs=[ospec, ospec, cspec, cspec],
        out_shape=[jax.ShapeDtypeStruct((bd, t_new, gw), F32)] * 2 + [jax.ShapeDtypeStruct((bd, gw, w), F32)] * 2,
        compiler_params=_params("arbitrary"),
        name=f"attn_sample_g{group}",
    )(proj, proj, proj, cache_k, cache_v)


def _hgrn_levels(chunk):
    levels, m = [], HG_SUB
    while m < chunk:
        levels.append(m)
        m *= 2
    return tuple(levels)


def _hgrn_kernel(zf_ref, zi_ref, zq_ref, zg_ref, lbraw_ref, gn_ref, s0_ref, o_ref, sfin_ref, st_ref, *, layer):
    nseq, chunk = zf_ref.shape[0], zf_ref.shape[1]
    chains = [(sq, hh) for sq in range(nseq) for hh in range(HG_HEADS_PER_STEP)]
    levels = _hgrn_levels(chunk)
    cidx = pl.program_id(2)
    dk, dv = HG_DK, HG_DV

    @pl.when(cidx == 0)
    def _():
        for sq, hh in chains:
            st_ref[sq, hh] = s0_ref[sq, hh].T

    raw = lbraw_ref[...]
    e = jnp.exp(raw - jnp.max(raw, axis=0, keepdims=True))
    lb_all = jnp.sum(e[0:layer + 1], axis=0, keepdims=True) / jnp.sum(e, axis=0, keepdims=True)

    rr = lax.broadcasted_iota(jnp.int32, (chunk, chunk), 0)
    cc = lax.broadcasted_iota(jnp.int32, (chunk, chunk), 1)
    sels = [cc <= rr] + [cc <= _div(rr, 2 * m) * (2 * m) + m - 1 for m in levels]
    sel = jnp.concatenate([jnp.where(sm, 1.0, 0.0) for sm in sels], axis=0).astype(BF16)

    trow = lax.broadcasted_iota(jnp.int32, (chunk, dk), 0)
    for sq, hh in chains:
        cs = slice(hh * dk, (hh + 1) * dk)
        lb = lb_all[:, cs]
        f = lb + (1.0 - lb) * jax.nn.sigmoid(zf_ref[sq][:, cs])
        lf = jnp.log(f)
        kh = 1.0 - f
        q = zq_ref[sq][:, cs]
        v = zi_ref[sq][:, cs]

        hi = lf.astype(BF16)
        r1 = lf - hi.astype(F32)
        mid = r1.astype(BF16)
        lo = (r1 - mid.astype(F32)).astype(BF16)
        bb = jnp.dot(sel, jnp.concatenate([hi, mid, lo], axis=1), preferred_element_type=F32)
        bb = bb[:, 0:dk] + bb[:, dk:2 * dk] + bb[:, 2 * dk:3 * dk]
        b = bb[0:chunk]

        st = st_ref[sq, hh]
        o = _bdot_nt(q * jnp.exp(b), st)

        for off in range(HG_SUB):
            if off == 0:
                a = jnp.sum(q * kh, axis=-1, keepdims=True)
                o = o + a * v
            else:
                in_block = _mod(trow, HG_SUB) >= off
                dec = jnp.exp(jnp.where(in_block, b - pltpu.roll(b, off, 0), NEG_INF))
                a = jnp.sum(q * pltpu.roll(kh, off, 0) * dec, axis=-1, keepdims=True)
                o = o + a * pltpu.roll(v, off, 0)

        if levels:
            ti = lax.broadcasted_iota(jnp.int32, (chunk, chunk), 0)
            si = lax.broadcasted_iota(jnp.int32, (chunk, chunk), 1)
            amat = jnp.zeros((chunk, chunk), F32)
            for li, m in enumerate(levels):
                bm = bb[(li + 1) * chunk:(li + 2) * chunk]
                second = _mod(_div(trow, m), 2) == 1
                qs = q * jnp.exp(jnp.where(second, b - bm, NEG_INF))
                ks = kh * jnp.exp(jnp.where(second, NEG_INF, bm - b))
                amat = amat + jnp.where(_div(ti, 2 * m) == _div(si, 2 * m), _bdot_nt(qs, ks), 0.0)
            o = o + _bdot(amat, v)

        b_end = b[chunk - 1:chunk]
        kd = kh * jnp.exp(b_end - b)
        vt = _pad_rows(v, dv).T
        st_ref[sq, hh] = st * jnp.exp(b_end) + _bdot(vt, _pad_rows(kd, dv))

        zg = zg_ref[sq][:, cs]
        o_ref[sq, :, cs] = _rms(o) * gn_ref[...] * (zg * jax.nn.sigmoid(zg))

    @pl.when(cidx == pl.num_programs(2) - 1)
    def _():
        for sq, hh in chains:
            sfin_ref[sq, hh] = st_ref[sq, hh].T


def _hgrn(proj, hg_lower_bound, hg_norm_g, s0, layer):
    b, l, _ = proj.shape
    assert HG_DK == HG_DV and HG_DV == LANES
    chunk = min(HG_CHUNK, l)
    assert l % chunk == 0 and chunk % HG_SUB == 0
    hps = HG_HEADS_PER_STEP
    wblk = hps * HG_DK
    bs = HG_SEQS_PER_STEP
    while b % bs:
        bs //= 2

    def pspec(col):
        return pl.BlockSpec((bs, chunk, wblk), lambda bb, hp, c: (bb, c, col + hp))

    sspec = pl.BlockSpec((bs, hps, HG_DK, HG_DV), lambda bb, hp, c: (bb, hp, 0, 0))
    return pl.pallas_call(
        functools.partial(_hgrn_kernel, layer=layer),
        grid=(b // bs, HG_HEADS // hps, l // chunk),
        in_specs=[pspec(COL_ZF), pspec(COL_ZI), pspec(COL_ZQ), pspec(COL_ZG),
                  pl.BlockSpec((hg_lower_bound.shape[0], wblk), lambda bb, hp, c: (0, hp)),
                  pl.BlockSpec((1, HG_DV), lambda bb, hp, c: (0, 0)),
                  sspec],
        out_specs=[pl.BlockSpec((bs, chunk, wblk), lambda bb, hp, c: (bb, c, hp)), sspec],
        out_shape=[jax.ShapeDtypeStruct((b, l, HG_HEADS * HG_DV), F32),
                   jax.ShapeDtypeStruct(s0.shape, F32)],
        scratch_shapes=[pltpu.VMEM((bs, hps, HG_DV, HG_DK), F32)],
        compiler_params=_params("arbitrary", "arbitrary", "arbitrary"),
        name="hgrn2",
    )(proj, proj, proj, proj, hg_lower_bound, hg_norm_g.reshape(1, HG_DV), s0)


def _merge_kernel(x_ref, o0_ref, l0_ref, o1_ref, l1_ref, o2_ref, l2_ref, oh_ref, za_ref, zh_ref,
                  g1_ref, sh2_ref, sc2_ref, n2_ref, wa_ref, wh_ref, wo_ref, x1_ref, h2_ref):
    nb, lb, d = x_ref.shape
    tm = nb * lb

    def flat(ref):
        return ref[...].reshape(tm, ref.shape[-1])

    l0, l1, l2 = flat(l0_ref), flat(l1_ref), flat(l2_ref)
    m = jnp.maximum(jnp.maximum(l0, l1), l2)
    e0, e1, e2 = jnp.exp(l0 - m), jnp.exp(l1 - m), jnp.exp(l2 - m)
    att = (e0 * flat(o0_ref) + e1 * flat(o1_ref) + e2 * flat(o2_ref)) / (e0 + e1 + e2)
    merged = (jax.nn.sigmoid(flat(za_ref)) * _bdot(att, wa_ref[...])
              + jax.nn.sigmoid(flat(zh_ref)) * _bdot(flat(oh_ref), wh_ref[...]))
    mix = _bdot(merged, wo_ref[...])
    x1 = x_ref[...] + g1_ref[...] * mix.reshape(nb, lb, d)
    x1_ref[...] = x1
    h2 = _rms(x1) * n2_ref[...] * (1.0 + sc2_ref[...]) + sh2_ref[...]
    h2_ref[...] = h2.reshape(tm, d).astype(BF16)


def _merge(x, att_o, att_lse, o_h, proj, mod3, norm2_g, wa, wh, wo):
    n, l, d = x.shape
    nb, lb = _token_blocks(n, l)
    gw = ATT_GROUP_WIDTH
    tpb = l // lb

    def tok(width, col=0):
        return pl.BlockSpec((nb, lb, width), lambda i, t: (i, t, col))

    def modspec(col):
        return pl.BlockSpec((nb, 1, d), lambda i, t: (i, 0, col))

    def full(a):
        return pl.BlockSpec(a.shape, lambda i, t: (0, 0))

    att_specs, att_args = [], []
    for o, lse in zip(att_o, att_lse):
        att_specs += [tok(gw), tok(gw)]
        att_args += [o, lse]
    return pl.pallas_call(
        _merge_kernel,
        grid=(n // nb, tpb),
        in_specs=[tok(d)] + att_specs + [tok(o_h.shape[-1]), tok(d, COL_ZA), tok(d, COL_ZH),
                                         modspec(2), modspec(3), modspec(4),
                                         pl.BlockSpec((1, d), lambda i, t: (0, 0)), full(wa), full(wh), full(wo)],
        out_specs=[tok(d), pl.BlockSpec((nb * lb, d), lambda i, t: (i * tpb + t, 0))],
        out_shape=[jax.ShapeDtypeStruct((n, l, d), F32), jax.ShapeDtypeStruct((n * l, d), BF16)],
        compiler_params=_params("arbitrary", "arbitrary"),
        name="merge",
    )(x, *att_args, o_h, proj, proj, mod3, mod3, mod3, norm2_g.reshape(1, d), wa, wh, wo)


def _top16(arrays):
    krow = lax.broadcasted_iota(jnp.int32, (PEER_TOPK, LANES), 0)

    def body(i, carry):
        out = []
        for s, vals, idxs in carry:
            rows = s.shape[0]
            row = lax.broadcasted_iota(jnp.int32, s.shape, 0)
            m = jnp.max(s, axis=0, keepdims=True)
            idx = jnp.min(jnp.where(s == m, row, rows), axis=0, keepdims=True)
            s = jnp.where(row == idx, NEG_INF, s)
            vals = jnp.where(krow == i, m, vals)
            idxs = jnp.where(krow == i, idx, idxs)
            out.append((s, vals, idxs))
        return tuple(out)

    init = tuple((s, jnp.zeros((PEER_TOPK, LANES), F32), jnp.zeros((PEER_TOPK, LANES), jnp.int32)) for s in arrays)
    return [(vals, idxs) for _, vals, idxs in lax.fori_loop(0, PEER_TOPK, body, init)]


def _bitonic_merge_desc(v):
    n = len(v)
    v = list(v)
    j = n // 2
    while j >= 1:
        for i in range(n):
            if i & j == 0:
                v[i], v[i + j] = jnp.maximum(v[i], v[i + j]), jnp.minimum(v[i], v[i + j])
        j //= 2
    return v


def _top16_values(s):
    kk = PEER_TOPK
    rows = s.shape[0]
    assert rows % SUBLANES == 0 and rows <= kk * SUBLANES
    v = [s[SUBLANES * i:SUBLANES * (i + 1)] for i in range(rows // SUBLANES)]
    v += [jnp.full((SUBLANES, LANES), NEG_INF, F32)] * (kk - len(v))
    size = 2
    while size <= kk:
        j = size // 2
        while j >= 1:
            for i in range(kk):
                if i & j == 0:
                    hi, lo = jnp.maximum(v[i], v[i + j]), jnp.minimum(v[i], v[i + j])
                    v[i], v[i + j] = (hi, lo) if i & size == 0 else (lo, hi)
            j //= 2
        size *= 2
    for shift in (1, 2, 4):
        other = [pltpu.roll(x, SUBLANES - shift, 0) for x in v]
        v = _bitonic_merge_desc([jnp.maximum(v[i], other[kk - 1 - i]) for i in range(kk)])
    return jnp.concatenate([x[0:1] for x in v], axis=0)


def _count_ge(s, threshold):
    return jnp.sum(jnp.where(s >= threshold, 1.0, 0.0), axis=0, keepdims=True)


def _peer_select_kernel(h2_ref, wq_ref, sk_ref, r2_ref, t1_ref, a1_ref, a2_ref, s_ref):
    tm = h2_ref.shape[0]
    kk = PEER_TOPK
    nk = PEER_NKEYS
    qry = jnp.dot(h2_ref[...], wq_ref[...], preferred_element_type=F32).astype(BF16)
    for h in range(PEER_HEADS):
        for p in range(2):
            c0 = (2 * h + p) * PEER_DHALF
            st = _bdot_nt(sk_ref[h, p], qry[:, c0:c0 + PEER_DHALF])
            for c in range(tm // LANES):
                s_ref[2 * h + p, c] = st[:, c * LANES:(c + 1) * LANES]

    row = lax.broadcasted_iota(jnp.int32, (nk, LANES), 0)
    krow = lax.broadcasted_iota(jnp.int32, (kk, LANES), 0)

    def candidates(v1, v2):
        return jnp.concatenate([v1[0:1] + v2] + [v1[a:a + 1] + v2[0:SUBLANES] for a in range(1, SUBLANES)]
                               + [v1[SUBLANES:kk] + v2[0:1]], axis=0)

    def finish(h, c, v1, i1, v2, i2, sf, ci):
        first = jnp.where(ci < kk, 0, jnp.where(ci < kk + 7 * SUBLANES, 1 + _div(ci - kk, SUBLANES),
                                                ci - (kk + 7 * SUBLANES) + SUBLANES))
        z = jnp.sum(jnp.exp(sf - sf[0:1]), axis=0, keepdims=True)
        count = jnp.zeros((kk, LANES), F32)
        for k in range(kk):
            count = count + jnp.where(krow == first[k:k + 1], 1.0, 0.0)
        t1 = jnp.zeros((nk, LANES), F32)
        r2 = jnp.full((nk, LANES), float(kk), F32)
        for a in range(kk):
            t1 = jnp.where(row == i1[a:a + 1], count[a:a + 1], t1)
            r2 = jnp.where(row == i2[a:a + 1], float(a), r2)
        r2_ref[h, c] = r2
        t1_ref[h, c] = t1
        a1_ref[h, c] = jnp.exp(s_ref[2 * h, c] - v1[0:1]) / z
        a2_ref[h, c] = jnp.exp(s_ref[2 * h + 1, c] - v2[0:1])

    def exact_pair(c, h0):
        halves = []
        for h in (h0, h0 + 1):
            (v1, i1), (v2, i2) = _top16([s_ref[2 * h, c], s_ref[2 * h + 1, c]])
            halves.append((v1, i1, v2, i2))
        picks = _top16([candidates(v1, v2) for v1, _, v2, _ in halves])
        for h, half, (sf, ci) in zip((h0, h0 + 1), halves, picks):
            finish(h, c, *half, sf, ci)

    def distinct_pair(c, h0):
        heads = (h0, h0 + 1)
        scores = [(s_ref[2 * h, c], s_ref[2 * h + 1, c]) for h in heads]
        tops = [(_top16_values(s1), _top16_values(s2)) for s1, s2 in scores]
        cands = [candidates(v1, v2) for v1, v2 in tops]
        sums = [_top16_values(cand) for cand in cands]
        doubtful = jnp.zeros((1, LANES), F32)
        for h, (s1, s2), (v1, v2), cand, sf in zip(heads, scores, tops, cands, sums):
            picked = jnp.where(cand >= sf[kk - 1:kk], 1.0, 0.0)
            repeats = sum(jnp.sum(jnp.where(v[0:kk - 1] == v[1:kk], 1.0, 0.0), axis=0, keepdims=True)
                          for v in (v1, v2, sf))
            sure = ((_count_ge(s1, v1[kk - 1:kk]) == kk) & (_count_ge(s2, v2[kk - 1:kk]) == kk)
                    & (jnp.sum(picked, axis=0, keepdims=True) == kk) & (repeats == 0.0))
            doubtful = doubtful + jnp.where(sure, 0.0, 1.0)
            count = jnp.concatenate(
                [jnp.sum(picked[0:kk], axis=0, keepdims=True)]
                + [jnp.sum(picked[kk + (a - 1) * SUBLANES:kk + a * SUBLANES], axis=0, keepdims=True)
                   for a in range(1, SUBLANES)]
                + [picked[kk + 7 * SUBLANES:]], axis=0)
            z = jnp.sum(jnp.exp(sf - sf[0:1]), axis=0, keepdims=True)
            t1 = jnp.zeros((nk, LANES), F32)
            r2 = jnp.full((nk, LANES), float(kk), F32)
            for a in range(kk):
                t1 = jnp.where(s1 == v1[a:a + 1], count[a:a + 1], t1)
                r2 = jnp.where(s2 == v2[a:a + 1], float(a), r2)
            r2_ref[h, c] = r2
            t1_ref[h, c] = t1
            a1_ref[h, c] = jnp.exp(s1 - v1[0:1]) / z
            a2_ref[h, c] = jnp.exp(s2 - v2[0:1])
        return jnp.sum(doubtful)

    def chunk_body(c, carry):
        for h0 in range(0, PEER_HEADS, 2):
            doubtful = distinct_pair(c, h0)

            @pl.when(doubtful > 0.0)
            def _():
                exact_pair(c, h0)
        return carry

    lax.fori_loop(0, tm // LANES, chunk_body, 0)


def _peer_select(h2, wq, subkeys):
    t, d = h2.shape
    tm = min(t, 512)
    assert t % tm == 0 and tm % LANES == 0
    nch = tm // LANES
    ospec = pl.BlockSpec((PEER_HEADS, nch, PEER_NKEYS, LANES), lambda i: (0, i, 0, 0))

    def shape(dtype):
        return jax.ShapeDtypeStruct((PEER_HEADS, t // LANES, PEER_NKEYS, LANES), dtype)

    return pl.pallas_call(
        _peer_select_kernel,
        grid=(t // tm,),
        in_specs=[pl.BlockSpec((tm, d), lambda i: (i, 0)),
                  pl.BlockSpec(wq.shape, lambda i: (0, 0)),
                  pl.BlockSpec(subkeys.shape, lambda i: (0, 0, 0, 0))],
        out_specs=[ospec] * 4,
        out_shape=[shape(F32)] * 4,
        scratch_shapes=[pltpu.VMEM((2 * PEER_HEADS, nch, PEER_NKEYS, LANES), F32)],
        compiler_params=_params("arbitrary"),
        name="peer_select",
    )(h2, wq, subkeys)


def _gelu_tanh(x):
    return x * (0.5 * (1.0 + jnp.tanh(0.7978845608028654 * (x + 0.044715 * (x * x * x)))))


def _peer_apply_kernel(h2_ref, u_ref, vt_ref, r2_ref, t1_ref, a1_ref, a2_ref, x1_ref, g2_ref, fn_ref,
                       y_ref, acc_ref, act_ref, gw_ref, r2b_ref, a2b_ref):
    nb, lb, d = x1_ref.shape
    tm = nb * lb
    nch = tm // LANES
    nk = PEER_NKEYS
    ne1 = PEER_E1_PER_STEP
    groups = nk // BF16_ROWS
    j = pl.program_id(2)

    @pl.when(j == 0)
    def _():
        acc_ref[...] = jnp.zeros_like(acc_ref)
        for h in range(PEER_HEADS):
            for tc in range(tm // LANES):
                r2b_ref[h, tc] = r2_ref[h, tc].astype(BF16)
                a2b_ref[h, tc] = a2_ref[h, tc].astype(BF16)

    for r0 in range(0, ne1 * nk, PEER_ACT_ROWS):
        rows = slice(r0, r0 + PEER_ACT_ROWS)
        act_ref[rows, :] = _gelu_tanh(lax.dot_general(u_ref[rows, :], h2_ref[...], (((1,), (1,)), ((), ())),
                                                      preferred_element_type=F32)).astype(BF16)

    def bcast(tile, e):
        return jnp.broadcast_to(tile[e:e + 1], (BF16_ROWS, LANES)).astype(BF16)

    zero = jnp.zeros((BF16_ROWS, LANES), BF16)
    for tc in range(nch):
        ls = slice(tc * LANES, (tc + 1) * LANES)
        for e in range(ne1):
            wsum = [zero] * groups
            for h in range(PEER_HEADS):
                t1 = bcast(t1_ref[h, tc], e)
                a1 = bcast(a1_ref[h, tc], e)
                for g in range(groups):
                    gs = slice(g * BF16_ROWS, (g + 1) * BF16_ROWS)
                    wsum[g] = wsum[g] + jnp.where(r2b_ref[h, tc, gs, :] < t1, a1 * a2b_ref[h, tc, gs, :], zero)
            for g in range(groups):
                rs = slice(e * nk + g * BF16_ROWS, e * nk + (g + 1) * BF16_ROWS)
                gw_ref[rs, ls] = wsum[g] * act_ref[rs, ls]
    acc_ref[...] += jnp.dot(vt_ref[...], gw_ref[...], preferred_element_type=F32)

    @pl.when(j == pl.num_programs(2) - 1)
    def _():
        ff = acc_ref[...].T.reshape(nb, lb, d)
        y_ref[...] = _rms(x1_ref[...] + g2_ref[...] * ff) * fn_ref[...]


def _peer_apply(h2, u_bf16, vt_bf16, sel, x1, mod3, final_g):
    n, l, d = x1.shape
    nb, lb = _token_blocks(n, l)
    tm = nb * lb
    tpb = l // lb
    ne1 = PEER_E1_PER_STEP
    assert ne1 % SUBLANES == 0 and tm % LANES == 0
    te = ne1 * PEER_NKEYS
    nch = tm // LANES
    selspec = pl.BlockSpec((PEER_HEADS, nch, PEER_NKEYS, LANES), lambda i, t, j: (0, i * tpb + t, 0, 0))
    rowspec = pl.BlockSpec((PEER_HEADS, nch, ne1, LANES), lambda i, t, j: (0, i * tpb + t, j, 0))
    tok = pl.BlockSpec((nb, lb, d), lambda i, t, j: (i, t, 0))
    return pl.pallas_call(
        _peer_apply_kernel,
        grid=(n // nb, tpb, PEER_NKEYS // ne1),
        in_specs=[pl.BlockSpec((tm, d), lambda i, t, j: (i * tpb + t, 0)),
                  pl.BlockSpec((te, d), lambda i, t, j: (j, 0)),
                  pl.BlockSpec((d, te), lambda i, t, j: (0, j)),
                  selspec, rowspec, rowspec, selspec,
                  tok,
                  pl.BlockSpec((nb, 1, d), lambda i, t, j: (i, 0, 5)),
                  pl.BlockSpec((1, d), lambda i, t, j: (0, 0))],
        out_specs=tok,
        out_shape=jax.ShapeDtypeStruct((n, l, d), F32),
        scratch_shapes=[pltpu.VMEM((d, tm), F32),
                        pltpu.VMEM((te, tm), BF16), pltpu.VMEM((te, tm), BF16),
                        pltpu.VMEM((PEER_HEADS, nch, PEER_NKEYS, LANES), BF16),
                        pltpu.VMEM((PEER_HEADS, nch, PEER_NKEYS, LANES), BF16)],
        compiler_params=_params("arbitrary", "arbitrary", "arbitrary"),
        name="peer_apply",
    )(h2, u_bf16, vt_bf16, *sel, x1, mod3, final_g.reshape(1, d))


def _to_rows(buf):
    n, _, w = buf.shape
    return buf.reshape(n, ATT_HEADS_PER_GROUP, ATT_HEAD_DIM, w).transpose(0, 3, 1, 2)


def _to_cols(buf):
    n, w = buf.shape[0], buf.shape[1]
    return buf.transpose(0, 2, 3, 1).reshape(n, ATT_GROUP_WIDTH, w)


def _layer(x, mod, w, caches, s0, layer, final_g):
    n, l, d = x.shape
    mod3 = mod.reshape(n, 1, N_MOD * d)
    proj = _inproj(x, mod3, w["norm1_g"], w["w_in"])
    att_o, att_lse, bufs = [], [], []
    for g, (window, _) in enumerate(ATT_GROUPS):
        if caches is None:
            o, lse = _attn_prompt(proj, g)
            keep = min(window, l)
            k0 = (COL_K + g) * ATT_GROUP_WIDTH
            v0 = (COL_V + g) * ATT_GROUP_WIDTH
            kb = proj[:, l - keep:, k0:k0 + ATT_GROUP_WIDTH].reshape(n, keep, ATT_HEADS_PER_GROUP, ATT_HEAD_DIM)
            vb = proj[:, l - keep:, v0:v0 + ATT_GROUP_WIDTH].reshape(n, keep, ATT_HEADS_PER_GROUP, ATT_HEAD_DIM)
        else:
            o, lse, kb, vb = _attn_sample(proj, _to_cols(caches[2 * g]), _to_cols(caches[2 * g + 1]), g)
            kb, vb = _to_rows(kb), _to_rows(vb)
        att_o.append(o)
        att_lse.append(lse)
        bufs += [kb, vb]
    o_h, s_fin = _hgrn(proj, w["hg_lower_bound"], w["hg_norm_g"], s0, layer)
    x1, h2 = _merge(x, att_o, att_lse, o_h, proj, mod3, w["norm2_g"], w["w_att_branch"], w["w_hg_branch"], w["w_out"])
    sel = _peer_select(h2, w["peer_wq"], w["peer_subkeys"])
    y = _peer_apply(h2, w["peer_u"], w["peer_vt"], sel, x1, mod3, final_g)
    return y, bufs, s_fin


def kernel(x_prompt, x_sample, cache_k0, cache_v0, cache_k1, cache_v1, cache_k2, cache_v2, state_hgrn, c_prompt, c_sample, ada_w, ada_b, norm1_g, w_in, w_att_branch, w_hg_branch, w_out, hg_lower_bound, hg_norm_g, norm2_g, peer_wq, peer_subkeys, peer_u, peer_v, final_norm_g):
    depth = ada_w.shape[0]
    assert depth == 1, "the final RMSNorm is fused into the last PEER kernel of a single layer"
    n_p = x_prompt.shape[0]
    layer = 0
    w = {
        "norm1_g": norm1_g[layer],
        "w_in": jnp.concatenate([w_in[layer][:, ORIG_GATE_START:], w_in[layer][:, :ORIG_GATE_START]], axis=1).astype(BF16),
        "w_att_branch": w_att_branch[layer].astype(BF16),
        "w_hg_branch": w_hg_branch[layer].astype(BF16),
        "w_out": w_out[layer].astype(BF16),
        "hg_lower_bound": hg_lower_bound,
        "hg_norm_g": hg_norm_g[layer],
        "norm2_g": norm2_g[layer],
        "peer_wq": peer_wq[layer].astype(BF16),
        "peer_subkeys": peer_subkeys[layer].astype(BF16),
        "peer_u": peer_u[layer].astype(BF16),
        "peer_vt": peer_v[layer].astype(BF16).T,
    }
    mod = _adaln(jnp.concatenate([c_prompt, c_sample], axis=0), ada_w[layer], ada_b[layer])
    s0_p = jnp.zeros((n_p,) + state_hgrn.shape[2:], F32)
    y_p, bufs_p, st_p = _layer(x_prompt, mod[:n_p], w, None, s0_p, layer, final_norm_g)
    caches = (cache_k0[layer], cache_v0[layer], cache_k1[layer], cache_v1[layer], cache_k2[layer], cache_v2[layer])
    y_s, bufs_s, st_s = _layer(x_sample, mod[n_p:], w, caches, state_hgrn[layer], layer, final_norm_g)
    return (y_p, y_s, *[b[None] for b in bufs_p], st_p[None], *[b[None] for b in bufs_s], st_s[None])
```
